```python
import jax, jax.numpy as jnp
from jax import lax
import numpy as np

D_MODEL = 1024
BATCH = 8
SEQ = 2048
DEPTH = 4
DEC_BATCH = 128
DEC_SEQ = 8
PAST_LEN = 16384
PAGE_SIZE = 128

N_META = 16
N_MIXERS = 2
N_A = (DEPTH + 1) // 2
N_B = DEPTH // 2
D_CONV = D_MODEL
CONV_W = 31
POOL_WINDOWS = (2, 4, 8, 16)
N_GROUPS = len(POOL_WINDOWS)
GROUP_DIM = D_MODEL // N_GROUPS
POOL_MAX = max(POOL_WINDOWS)
D_FF = 2816
FFN_CONV_W = 3
EPS = 1e-6

kernel_name = "hybrid_conv_pool_decoder_step"


def rmsnorm(x, g):
    xf = x.astype(jnp.float32)
    y = xf * lax.rsqrt(jnp.mean(xf * xf, axis=-1, keepdims=True) + EPS)
    return (y * g.astype(jnp.float32)).astype(x.dtype)


def layernorm(x, g, b):
    xf = x.astype(jnp.float32)
    mu = jnp.mean(xf, axis=-1, keepdims=True)
    var = jnp.mean(jnp.square(xf - mu), axis=-1, keepdims=True)
    y = (xf - mu) * lax.rsqrt(var + EPS)
    return (y * g.astype(jnp.float32) + b.astype(jnp.float32)).astype(x.dtype)


def causal_dwconv(x, prefix, w, b):
    width = w.shape[0]
    xp = jnp.concatenate([prefix.astype(x.dtype), x], axis=1)
    out = lax.conv_general_dilated(
        xp, w[:, None, :].astype(x.dtype), window_strides=(1,), padding="VALID",
        dimension_numbers=("NWC", "WIO", "NWC"), feature_group_count=x.shape[-1])
    return out + b.astype(x.dtype), xp[:, xp.shape[1] - (width - 1):]


def conv_mixer(h, prefix, w_in, b_in, w_dw, b_dw, ln_g, ln_b, w_out, b_out):
    a = h @ w_in + b_in
    u, g = jnp.split(a, 2, axis=-1)
    v = u * jax.nn.sigmoid(g)
    c, new_prefix = causal_dwconv(v, prefix, w_dw, b_dw)
    c = jax.nn.silu(layernorm(c, ln_g, ln_b))
    return c @ w_out + b_out, new_prefix


def pool_mixer(h, prefix, start_pos, w_grp, scale):
    bsz, t_len, d = h.shape
    p = POOL_MAX - 1
    hp = jnp.concatenate([prefix.astype(h.dtype), h], axis=1)
    cs = jnp.concatenate([jnp.zeros((bsz, 1, d), jnp.float32),
                          jnp.cumsum(hp.astype(jnp.float32), axis=1)], axis=1)
    pos = start_pos + jnp.arange(t_len)
    parts = []
    for gi, w in enumerate(POOL_WINDOWS):
        sl = slice(gi * GROUP_DIM, (gi + 1) * GROUP_DIM)
        win = cs[:, p + 1:p + 1 + t_len, sl] - cs[:, p + 1 - w:p + 1 - w + t_len, sl]
        cnt = jnp.minimum(w, pos + 1).astype(jnp.float32)[None, :, None]
        parts.append(win / cnt)
    pooled = jnp.concatenate(parts, axis=-1)
    diff = (pooled - h.astype(jnp.float32)).astype(h.dtype).reshape(bsz, t_len, N_GROUPS, GROUP_DIM)
    y = jnp.einsum("btgc,gce->btge", diff, w_grp).reshape(bsz, t_len, d)
    return y * scale, hp[:, hp.shape[1] - p:]


def conv_ffn(h, prefix, w_up, w_dw, b_dw, w_down):
    u = h @ w_up
    c, new_prefix = causal_dwconv(u, prefix, w_dw, b_dw)
    gate, val = jnp.split(c, 2, axis=-1)
    return (jax.nn.silu(gate) * val) @ w_down, new_prefix


def trunk(x, st_conv, st_pool, st_ffn, start_pos, p):
    new_conv, new_pool, new_ffn = [], [], []
    for i in range(DEPTH):
        h = rmsnorm(x, p["mix_pre"][i])
        if i % N_MIXERS == 0:
            j = i // N_MIXERS
            m, ns = conv_mixer(h, st_conv[j], p["a_w_in"][j], p["a_b_in"][j], p["a_w_dw"][j],
                               p["a_b_dw"][j], p["a_ln_g"][j], p["a_ln_b"][j],
                               p["a_w_out"][j], p["a_b_out"][j])
            new_conv.append(ns)
        else:
            j = i // N_MIXERS
            m, ns = pool_mixer(h, st_pool[j], start_pos, p["b_w_grp"][j], p["b_scale"][j])
            new_pool.append(ns)
        x = x + rmsnorm(m, p["mix_post"][i])
        h = rmsnorm(x, p["ffn_pre"][i])
        f, ns = conv_ffn(h, st_ffn[i], p["f_w_up"][i], p["f_w_dw"][i], p["f_b_dw"][i], p["f_w_down"][i])
        new_ffn.append(ns)
        x = x + rmsnorm(f, p["ffn_post"][i])
    y = rmsnorm(x, p["final_norm"])
    return y, jnp.stack(new_conv), jnp.stack(new_pool), jnp.stack(new_ffn)


def setup_inputs(seed: int = 0) -> dict:
    key = jax.random.key(seed)
    ks = jax.random.split(key, 32)
    f32 = jnp.float32
    nrm = lambda k, shape, s=1.0: (jax.random.normal(k, shape, f32) * s)
    gain = lambda k, shape: 1.0 + 0.05 * jax.random.normal(k, shape, f32)
    return {
        "x_prompt": nrm(ks[0], (BATCH, SEQ, D_MODEL)),
        "x_sample": nrm(ks[1], (DEC_BATCH, DEC_SEQ, D_MODEL)),
        "state_conv": nrm(ks[2], (N_A, DEC_BATCH, CONV_W - 1, D_CONV), 0.5),
        "state_pool": nrm(ks[3], (N_B, DEC_BATCH, POOL_MAX - 1, D_MODEL)),
        "state_ffn": nrm(ks[4], (DEPTH, DEC_BATCH, FFN_CONV_W - 1, 2 * D_FF)),
        "meta_tokens": nrm(ks[5], (N_META, D_MODEL)),
        "mix_pre": gain(ks[6], (DEPTH, D_MODEL)),
        "mix_post": gain(ks[7], (DEPTH, D_MODEL)),
        "ffn_pre": gain(ks[8], (DEPTH, D_MODEL)),
        "ffn_post": gain(ks[9], (DEPTH, D_MODEL)),
        "final_norm": gain(ks[10], (D_MODEL,)),
        "a_w_in": nrm(ks[11], (N_A, D_MODEL, 2 * D_CONV), D_MODEL ** -0.5),
        "a_b_in": nrm(ks[12], (N_A, 2 * D_CONV), 0.02),
        "a_w_dw": nrm(ks[13], (N_A, CONV_W, D_CONV), CONV_W ** -0.5),
        "a_b_dw": nrm(ks[14], (N_A, D_CONV), 0.02),
        "a_ln_g": gain(ks[15], (N_A, D_CONV)),
        "a_ln_b": nrm(ks[16], (N_A, D_CONV), 0.02),
        "a_w_out": nrm(ks[17], (N_A, D_CONV, D_MODEL), D_CONV ** -0.5),
        "a_b_out": nrm(ks[18], (N_A, D_MODEL), 0.02),
        "b_w_grp": nrm(ks[19], (N_B, N_GROUPS, GROUP_DIM, GROUP_DIM), GROUP_DIM ** -0.5),
        "b_scale": gain(ks[20], (N_B, D_MODEL)),
        "f_w_up": nrm(ks[21], (DEPTH, D_MODEL, 2 * D_FF), D_MODEL ** -0.5),
        "f_w_dw": nrm(ks[22], (DEPTH, FFN_CONV_W, 2 * D_FF), FFN_CONV_W ** -0.5),
        "f_b_dw": nrm(ks[23], (DEPTH, 2 * D_FF), 0.02),
        "f_w_down": nrm(ks[24], (DEPTH, D_FF, D_MODEL), D_FF ** -0.5),
    }


def reference(x_prompt, x_sample, state_conv, state_pool, state_ffn, meta_tokens,
              mix_pre, mix_post, ffn_pre, ffn_post, final_norm,
              a_w_in, a_b_in, a_w_dw, a_b_dw, a_ln_g, a_ln_b, a_w_out, a_b_out,
              b_w_grp, b_scale, f_w_up, f_w_dw, f_b_dw, f_w_down):
    p = {"mix_pre": mix_pre, "mix_post": mix_post, "ffn_pre": ffn_pre, "ffn_post": ffn_post,
         "final_norm": final_norm, "a_w_in": a_w_in, "a_b_in": a_b_in, "a_w_dw": a_w_dw,
         "a_b_dw": a_b_dw, "a_ln_g": a_ln_g, "a_ln_b": a_ln_b, "a_w_out": a_w_out,
         "a_b_out": a_b_out, "b_w_grp": b_w_grp, "b_scale": b_scale, "f_w_up": f_w_up,
         "f_w_dw": f_w_dw, "f_b_dw": f_b_dw, "f_w_down": f_w_down}
    dt = x_prompt.dtype
    bsz = x_prompt.shape[0]
    meta = jnp.broadcast_to(meta_tokens.astype(dt)[None], (bsz, N_META, D_MODEL))
    xp = jnp.concatenate([meta, x_prompt], axis=1)
    z_conv = jnp.zeros((N_A, bsz, CONV_W - 1, D_CONV), dt)
    z_pool = jnp.zeros((N_B, bsz, POOL_MAX - 1, D_MODEL), dt)
    z_ffn = jnp.zeros((DEPTH, bsz, FFN_CONV_W - 1, 2 * D_FF), dt)
    yp, nc_p, np_p, nf_p = trunk(xp, z_conv, z_pool, z_ffn, 0, p)
    y_prompt = yp[:, N_META:]
    y_sample, nc_s, np_s, nf_s = trunk(x_sample, state_conv, state_pool, state_ffn, PAST_LEN, p)
    return (y_prompt, y_sample, nc_p, np_p, nf_p, nc_s, np_s, nf_s)
```

```python
import functools

import jax
import jax.numpy as jnp
from jax import lax
from jax.experimental import pallas as pl
from jax.experimental.pallas import tpu as pltpu

LANES = 128
SUBLANES = 8
VMEM_LIMIT_BYTES = 56 * 1024 * 1024

EPS = 1e-6
N_META = 16
PAST_LEN = 16384
CONV_W = 31
POOL_WINDOWS = (2, 4, 8, 16)
FFN_CONV_W = 3
CONV_CHUNK_ROWS = 32

F32 = jnp.float32
BF16 = jnp.bfloat16


def _round_up(n, m):
    return (n + m - 1) // m * m


def _rmsnorm(x, g):
    ms = jnp.mean(x * x, axis=-1, keepdims=True)
    return x * lax.rsqrt(ms + EPS) * g


def _chunks(bb, tm):
    rc = min(tm, CONV_CHUNK_ROWS)
    bc = min(bb, max(1, CONV_CHUNK_ROWS // rc))
    return bc, rc


def _conv_mixer_kernel(x_ref, st_ref, pre_ref, w_in_ref, b_in_ref, w_dw_ref, b_dw_ref,
                       ln_g_ref, ln_b_ref, w_out_ref, b_out_ref, post_ref,
                       y_ref, nst_ref, xp_ref, c_ref, wb_ref, *, carry):
    bb, tm, d = x_ref.shape
    c_dim = w_out_ref.shape[0]
    nj = c_dim // LANES
    p = CONV_W - 1
    pp = _round_up(p, SUBLANES)
    m = bb * tm
    t = pl.program_id(1)

    @pl.when((pl.program_id(0) == 0) & (t == 0))
    def _():
        for k in range(CONV_W):
            wb_ref[k] = jnp.broadcast_to(w_dw_ref[k:k + 1, :], (SUBLANES, c_dim))

    @pl.when(t == 0)
    def _():
        for j in range(nj):
            xp_ref[:, j, pp - p:pp, :] = st_ref[:, :, j * LANES:(j + 1) * LANES]

    x = x_ref[...].reshape(m, d)
    h = _rmsnorm(x, pre_ref[...])
    a = jnp.dot(h.astype(BF16), w_in_ref[...], preferred_element_type=F32) + b_in_ref[...]
    v = a[:, :c_dim] * jax.nn.sigmoid(a[:, c_dim:])
    v3 = v.reshape(bb, tm, c_dim)
    for j in range(nj):
        xp_ref[:, j, pp:pp + tm, :] = v3[:, :, j * LANES:(j + 1) * LANES]

    bc, rc = _chunks(bb, tm)
    nv = bc * rc // SUBLANES
    for b0 in range(0, bb, bc):
        for r0 in range(0, tm, rc):
            for j in range(nj):
                lanes = slice(j * LANES, (j + 1) * LANES)
                acc = None
                for k in range(CONV_W):
                    lo = pp - p + k + r0
                    xs = xp_ref[b0:b0 + bc, j, lo:lo + rc, :].reshape(nv, SUBLANES, LANES)
                    term = xs * wb_ref[k, :, lanes][None]
                    acc = term if acc is None else acc + term
                acc = acc + jnp.broadcast_to(b_dw_ref[:, lanes], (SUBLANES, LANES))[None]
                c_ref[b0:b0 + bc, r0:r0 + rc, lanes] = acc.reshape(bc, rc, LANES)

    c = c_ref[...].reshape(m, c_dim)
    mu = jnp.mean(c, axis=-1, keepdims=True)
    cc = c - mu
    var = jnp.mean(cc * cc, axis=-1, keepdims=True)
    yl = cc * lax.rsqrt(var + EPS) * ln_g_ref[...] + ln_b_ref[...]
    s = yl * jax.nn.sigmoid(yl)
    mo = jnp.dot(s.astype(BF16), w_out_ref[...], preferred_element_type=F32) + b_out_ref[...]
    out = x + _rmsnorm(mo, post_ref[...])
    y_ref[...] = out.reshape(bb, tm, d)

    for j in range(nj):
        tail = xp_ref[:, j, tm + pp - p:tm + pp, :]
        nst_ref[:, :, j * LANES:(j + 1) * LANES] = tail
        if carry:
            xp_ref[:, j, pp - p:pp, :] = tail


def _pool_mixer_kernel(x_ref, st_ref, pre_ref, w_grp_ref, scale_ref, post_ref,
                       y_ref, nst_ref, xp_ref, d_ref, *, start_pos, carry):
    bb, tm, d = x_ref.shape
    nj = d // LANES
    ng = len(POOL_WINDOWS)
    gd = d // ng
    p = max(POOL_WINDOWS) - 1
    pp = _round_up(p, SUBLANES)
    m = bb * tm
    t = pl.program_id(1)

    @pl.when(t == 0)
    def _():
        for j in range(nj):
            xp_ref[:, j, pp - p:pp, :] = st_ref[:, :, j * LANES:(j + 1) * LANES]

    x = x_ref[...].reshape(m, d)
    h3 = _rmsnorm(x, pre_ref[...]).reshape(bb, tm, d)
    for j in range(nj):
        xp_ref[:, j, pp:pp + tm, :] = h3[:, :, j * LANES:(j + 1) * LANES]

    bc, rc = _chunks(bb, tm)
    for b0 in range(0, bb, bc):
        for r0 in range(0, tm, rc):
            pos = start_pos + t * tm + r0 + lax.broadcasted_iota(jnp.int32, (bc, rc, LANES), 1)
            for g, w in enumerate(POOL_WINDOWS):
                cnt = jnp.minimum(w, pos + 1).astype(F32)
                for j in range(g * gd // LANES, (g + 1) * gd // LANES):
                    win = None
                    for k in range(w):
                        lo = pp - k + r0
                        xs = xp_ref[b0:b0 + bc, j, lo:lo + rc, :]
                        win = xs if win is None else win + xs
                    cur = xp_ref[b0:b0 + bc, j, pp + r0:pp + r0 + rc, :]
                    d_ref[b0:b0 + bc, r0:r0 + rc, j * LANES:(j + 1) * LANES] = win / cnt - cur

    dd = d_ref[...].reshape(m, d).astype(BF16)
    ys = [jnp.dot(dd[:, g * gd:(g + 1) * gd], w_grp_ref[g], preferred_element_type=F32)
          for g in range(ng)]
    y = jnp.concatenate(ys, axis=-1) * scale_ref[...]
    out = x + _rmsnorm(y, post_ref[...])
    y_ref[...] = out.reshape(bb, tm, d)

    for j in range(nj):
        tail = xp_ref[:, j, tm + pp - p:tm + pp, :]
        nst_ref[:, :, j * LANES:(j + 1) * LANES] = tail
        if carry:
            xp_ref[:, j, pp - p:pp, :] = tail


def _ffn_kernel(*refs, nc, final):
    if final:
        (x_ref, st_ref, pre_ref, w_up_ref, w_dw_ref, b_dw_ref, w_down_ref, post_ref, fin_ref,
         y_ref, nst_ref, slab_ref, cr_ref, act_ref) = refs
    else:
        (x_ref, st_ref, pre_ref, w_up_ref, w_dw_ref, b_dw_ref, w_down_ref, post_ref,
         y_ref, nst_ref, slab_ref, cr_ref, act_ref) = refs
    bb, tm, d = x_ref.shape
    f = w_down_ref.shape[0]
    p = FFN_CONV_W - 1
    pp = SUBLANES
    m = bb * tm
    njc = nc // LANES
    t = pl.program_id(1)

    @pl.when(t == 0)
    def _():
        cr_ref[...] = st_ref[...]

    x = x_ref[...].reshape(m, d)
    h = _rmsnorm(x, pre_ref[...]).astype(BF16)
    for c0 in range(0, f, nc):
        halves = []
        for hi, base in enumerate((c0, f + c0)):
            u = jnp.dot(h, w_up_ref[:, base:base + nc], preferred_element_type=F32)
            tiles = []
            for jj in range(njc):
                lanes = slice(base + jj * LANES, base + (jj + 1) * LANES)
                slot = hi * njc + jj
                u_t = u[:, jj * LANES:(jj + 1) * LANES].reshape(bb, tm, LANES)
                slab_ref[:, slot, pp - p:pp, :] = cr_ref[:, :, lanes]
                slab_ref[:, slot, pp:pp + tm, :] = u_t
                um2 = slab_ref[:, slot, pp - 2:pp - 2 + tm, :]
                um1 = slab_ref[:, slot, pp - 1:pp - 1 + tm, :]
                cr_ref[:, :, lanes] = slab_ref[:, slot, tm + pp - p:tm + pp, :]
                cc = (um2 * w_dw_ref[0:1, lanes] + um1 * w_dw_ref[1:2, lanes]
                      + u_t * w_dw_ref[2:3, lanes] + b_dw_ref[:, lanes])
                tiles.append(cc.reshape(m, LANES))
            halves.append(jnp.concatenate(tiles, axis=-1))
        gate, val = halves
        act_ref[:, c0:c0 + nc] = (gate * jax.nn.sigmoid(gate) * val).astype(BF16)

    fo = jnp.dot(act_ref[...], w_down_ref[...], preferred_element_type=F32)
    out = x + _rmsnorm(fo, post_ref[...])
    if final:
        out = _rmsnorm(out, fin_ref[...])
    y_ref[...] = out.reshape(bb, tm, d)
    nst_ref[...] = cr_ref[...]


def _const_spec(shape):
    nd = len(shape)
    return pl.BlockSpec(shape, lambda b, t: (0,) * nd, pipeline_mode=pl.Buffered(1))


def _seq_specs(x, st, bb, tm):
    bsz, t_len, d = x.shape
    x_spec = pl.BlockSpec((bb, tm, d), lambda b, t: (b, t, 0))
    if st.shape[0] == bsz:
        st_spec = pl.BlockSpec((bb,) + st.shape[1:], lambda b, t: (b, 0, 0))
    else:
        assert st.shape[0] == 1 and bb == 1
        st_spec = pl.BlockSpec(st.shape, lambda b, t: (0, 0, 0))
    nst_spec = pl.BlockSpec((bb,) + st.shape[1:], lambda b, t: (b, 0, 0))
    nst_shape = jax.ShapeDtypeStruct((bsz,) + st.shape[1:], F32)
    return x_spec, st_spec, nst_spec, nst_shape


_PARAMS = pltpu.CompilerParams(dimension_semantics=("arbitrary", "arbitrary"),
                               vmem_limit_bytes=VMEM_LIMIT_BYTES)


def _conv_mixer(x, st, w, bb, tm):
    bsz, t_len, d = x.shape
    c_dim = w["w_out"].shape[0]
    x_spec, st_spec, nst_spec, nst_shape = _seq_specs(x, st, bb, tm)
    pp = _round_up(CONV_W - 1, SUBLANES)
    consts = [w["pre"], w["w_in"], w["b_in"], w["w_dw"], w["b_dw"], w["ln_g"], w["ln_b"],
              w["w_out"], w["b_out"], w["post"]]
    return pl.pallas_call(
        functools.partial(_conv_mixer_kernel, carry=t_len > tm),
        grid=(bsz // bb, t_len // tm),
        in_specs=[x_spec, st_spec] + [_const_spec(a.shape) for a in consts],
        out_specs=[x_spec, nst_spec],
        out_shape=[jax.ShapeDtypeStruct(x.shape, F32), nst_shape],
        scratch_shapes=[pltpu.VMEM((bb, c_dim // LANES, pp + tm, LANES), F32),
                        pltpu.VMEM((bb, tm, c_dim), F32),
                        pltpu.VMEM((CONV_W, SUBLANES, c_dim), F32)],
        compiler_params=_PARAMS,
        name="conv_mixer",
    )(x, st, *consts)


def _pool_mixer(x, st, w, bb, tm, start_pos):
    bsz, t_len, d = x.shape
    x_spec, st_spec, nst_spec, nst_shape = _seq_specs(x, st, bb, tm)
    pp = _round_up(max(POOL_WINDOWS) - 1, SUBLANES)
    consts = [w["pre"], w["w_grp"], w["scale"], w["post"]]
    return pl.pallas_call(
        functools.partial(_pool_mixer_kernel, start_pos=start_pos, carry=t_len > tm),
        grid=(bsz // bb, t_len // tm),
        in_specs=[x_spec, st_spec] + [_const_spec(a.shape) for a in consts],
        out_specs=[x_spec, nst_spec],
        out_shape=[jax.ShapeDtypeStruct(x.shape, F32), nst_shape],
        scratch_shapes=[pltpu.VMEM((bb, d // LANES, pp + tm, LANES), F32),
                        pltpu.VMEM((bb, tm, d), F32)],
        compiler_params=_PARAMS,
        name="pool_mixer",
    )(x, st, *consts)


def _conv_ffn(x, st, w, bb, tm, final_g=None):
    bsz, t_len, d = x.shape
    f = w["w_down"].shape[0]
    nc = 2 * LANES
    x_spec, st_spec, nst_spec, nst_shape = _seq_specs(x, st, bb, tm)
    consts = [w["pre"], w["w_up"], w["w_dw"], w["b_dw"], w["w_down"], w["post"]]
    if final_g is not None:
        consts.append(final_g)
    return pl.pallas_call(
        functools.partial(_ffn_kernel, nc=nc, final=final_g is not None),
        grid=(bsz // bb, t_len // tm),
        in_specs=[x_spec, st_spec] + [_const_spec(a.shape) for a in consts],
        out_specs=[x_spec, nst_spec],
        out_shape=[jax.ShapeDtypeStruct(x.shape, F32), nst_shape],
        scratch_shapes=[pltpu.VMEM((bb, 2 * nc // LANES, SUBLANES + tm, LANES), F32),
                        pltpu.VMEM((bb, FFN_CONV_W - 1, 2 * f), F32),
                        pltpu.VMEM((bb * tm, f), BF16)],
        compiler_params=_PARAMS,
        name="conv_ffn",
    )(x, st, *consts)


def _trunk(x, st_conv, st_pool, st_ffn, start_pos, layers, final_g, bb, tm):
    new_conv, new_pool, new_ffn = [], [], []
    depth = len(layers)
    for i, (mixer, ffn) in enumerate(layers):
        if i % 2 == 0:
            x, ns = _conv_mixer(x, st_conv[i // 2], mixer, bb, tm)
            new_conv.append(ns)
        else:
            x, ns = _pool_mixer(x, st_pool[i // 2], mixer, bb, tm, start_pos)
            new_pool.append(ns)
        x, ns = _conv_ffn(x, st_ffn[i], ffn, bb, tm, final_g if i == depth - 1 else None)
        new_ffn.append(ns)
    return x, new_conv, new_pool, new_ffn


def kernel(x_prompt, x_sample, state_conv, state_pool, state_ffn, meta_tokens, mix_pre, mix_post,
           ffn_pre, ffn_post, final_norm, a_w_in, a_b_in, a_w_dw, a_b_dw, a_ln_g, a_ln_b, a_w_out,
           a_b_out, b_w_grp, b_scale, f_w_up, f_w_dw, f_b_dw, f_w_down):
    depth = mix_pre.shape[0]
    d = x_prompt.shape[-1]
    row = lambda a: a.reshape(1, -1)
    layers = []
    for i in range(depth):
        j = i // 2
        if i % 2 == 0:
            mixer = dict(pre=row(mix_pre[i]), w_in=a_w_in[j].astype(BF16), b_in=row(a_b_in[j]),
                         w_dw=a_w_dw[j], b_dw=row(a_b_dw[j]), ln_g=row(a_ln_g[j]),
                         ln_b=row(a_ln_b[j]), w_out=a_w_out[j].astype(BF16),
                         b_out=row(a_b_out[j]), post=row(mix_post[i]))
        else:
            mixer = dict(pre=row(mix_pre[i]), w_grp=b_w_grp[j].astype(BF16),
                         scale=row(b_scale[j]), post=row(mix_post[i]))
        ffn = dict(pre=row(ffn_pre[i]), w_up=f_w_up[i].astype(BF16), w_dw=f_w_dw[i],
                   b_dw=row(f_b_dw[i]), w_down=f_w_down[i].astype(BF16), post=row(ffn_post[i]))
        layers.append((mixer, ffn))
    final_g = row(final_norm)
    n_a, n_b = state_conv.shape[0], state_pool.shape[0]

    zeros = lambda a: jnp.zeros((1,) + a.shape[2:], F32)
    _, mc, mp, mf = _trunk(meta_tokens[None].astype(F32),
                           [zeros(state_conv)] * n_a, [zeros(state_pool)] * n_b,
                           [zeros(state_ffn)] * depth, 0, layers, final_g, 1, N_META)
    y_prompt, pc, ppl, pf = _trunk(x_prompt, mc, mp, mf, N_META, layers, final_g, 1, 512)
    y_sample, sc, sp, sf = _trunk(x_sample, list(state_conv), list(state_pool), list(state_ffn),
                                  PAST_LEN, layers, final_g, 32, x_sample.shape[1])
    return (y_prompt, y_sample, jnp.stack(pc), jnp.stack(ppl), jnp.stack(pf),
            jnp.stack(sc), jnp.stack(sp), jnp.stack(sf))
```

```python
import functools

import jax
import jax.numpy as jnp
from jax import lax
from jax.experimental import pallas as pl
from jax.experimental.pallas import tpu as pltpu

LANES = 128
SUBLANES = 8
VMEM_LIMIT_BYTES = 56 * 1024 * 1024

EPS = 1e-6
N_META = 16
PAST_LEN = 16384
CONV_W = 31
POOL_WINDOWS = (2, 4, 8, 16)
FFN_CONV_W = 3
CONV_CHUNK_ROWS = 32

F32 = jnp.float32
BF16 = jnp.bfloat16


def _round_up(n, m):
    return (n + m - 1) // m * m


def _rmsnorm(x, g):
    ms = jnp.mean(x * x, axis=-1, keepdims=True)
    return x * lax.rsqrt(ms + EPS) * g


def _chunks(bb, tm):
    rc = min(tm, CONV_CHUNK_ROWS)
    bc = min(bb, max(1, CONV_CHUNK_ROWS // rc))
    return bc, rc


def _conv_mixer_kernel(x_ref, st_ref, pre_ref, w_in_ref, b_in_ref, w_dw_ref, b_dw_ref,
                       ln_g_ref, ln_b_ref, w_out_ref, b_out_ref, post_ref,
                       y_ref, nst_ref, xp_ref, c_ref, wb_ref, *, carry):
    bb, tm, d = x_ref.shape
    c_dim = w_out_ref.shape[0]
    nj = c_dim // LANES
    p = CONV_W - 1
    pp = _round_up(p, SUBLANES)
    m = bb * tm
    t = pl.program_id(1)

    @pl.when((pl.program_id(0) == 0) & (t == 0))
    def _():
        for k in range(CONV_W):
            wb_ref[k] = jnp.broadcast_to(w_dw_ref[k:k + 1, :], (SUBLANES, c_dim))

    @pl.when(t == 0)
    def _():
        for j in range(nj):
            xp_ref[:, j, pp - p:pp, :] = st_ref[0, :, :, j * LANES:(j + 1) * LANES]

    x = x_ref[...].reshape(m, d)
    h = _rmsnorm(x, pre_ref[...])
    a = jnp.dot(h.astype(BF16), w_in_ref[...], preferred_element_type=F32) + b_in_ref[...]
    v = a[:, :c_dim] * jax.nn.sigmoid(a[:, c_dim:])
    v3 = v.reshape(bb, tm, c_dim)
    for j in range(nj):
        xp_ref[:, j, pp:pp + tm, :] = v3[:, :, j * LANES:(j + 1) * LANES]

    bc, rc = _chunks(bb, tm)
    nv = bc * rc // SUBLANES
    for b0 in range(0, bb, bc):
        for r0 in range(0, tm, rc):
            for j in range(nj):
                lanes = slice(j * LANES, (j + 1) * LANES)
                acc = None
                for k in range(CONV_W):
                    lo = pp - p + k + r0
                    xs = xp_ref[b0:b0 + bc, j, lo:lo + rc, :].reshape(nv, SUBLANES, LANES)
                    term = xs * wb_ref[k, :, lanes][None]
                    acc = term if acc is None else acc + term
                acc = acc + jnp.broadcast_to(b_dw_ref[:, lanes], (SUBLANES, LANES))[None]
                c_ref[b0:b0 + bc, r0:r0 + rc, lanes] = acc.reshape(bc, rc, LANES)

    c = c_ref[...].reshape(m, c_dim)
    mu = jnp.mean(c, axis=-1, keepdims=True)
    cc = c - mu
    var = jnp.mean(cc * cc, axis=-1, keepdims=True)
    yl = cc * lax.rsqrt(var + EPS) * ln_g_ref[...] + ln_b_ref[...]
    s = yl * jax.nn.sigmoid(yl)
    mo = jnp.dot(s.astype(BF16), w_out_ref[...], preferred_element_type=F32) + b_out_ref[...]
    out = x + _rmsnorm(mo, post_ref[...])
    y_ref[...] = out.reshape(bb, tm, d)

    for j in range(nj):
        tail = xp_ref[:, j, tm + pp - p:tm + pp, :]
        nst_ref[0, :, :, j * LANES:(j + 1) * LANES] = tail
        if carry:
            xp_ref[:, j, pp - p:pp, :] = tail


def _pool_mixer_kernel(x_ref, st_ref, pre_ref, w_grp_ref, scale_ref, post_ref,
                       y_ref, nst_ref, xp_ref, d_ref, *, start_pos, carry):
    bb, tm, d = x_ref.shape
    nj = d // LANES
    ng = len(POOL_WINDOWS)
    gd = d // ng
    p = max(POOL_WINDOWS) - 1
    pp = _round_up(p, SUBLANES)
    m = bb * tm
    t = pl.program_id(1)

    @pl.when(t == 0)
    def _():
        for j in range(nj):
            xp_ref[:, j, pp - p:pp, :] = st_ref[0, :, :, j * LANES:(j + 1) * LANES]

    x = x_ref[...].reshape(m, d)
    h3 = _rmsnorm(x, pre_ref[...]).reshape(bb, tm, d)
    for j in range(nj):
        xp_ref[:, j, pp:pp + tm, :] = h3[:, :, j * LANES:(j + 1) * LANES]

    bc, rc = _chunks(bb, tm)
    for b0 in range(0, bb, bc):
        for r0 in range(0, tm, rc):
            pos = start_pos + t * tm + r0 + lax.broadcasted_iota(jnp.int32, (bc, rc, LANES), 1)
            for g, w in enumerate(POOL_WINDOWS):
                cnt = jnp.minimum(w, pos + 1).astype(F32)
                for j in range(g * gd // LANES, (g + 1) * gd // LANES):
                    win = None
                    for k in range(w):
                        lo = pp - k + r0
                        xs = xp_ref[b0:b0 + bc, j, lo:lo + rc, :]
                        win = xs if win is None else win + xs
                    cur = xp_ref[b0:b0 + bc, j, pp + r0:pp + r0 + rc, :]
                    d_ref[b0:b0 + bc, r0:r0 + rc, j * LANES:(j + 1) * LANES] = win / cnt - cur

    dd = d_ref[...].reshape(m, d).astype(BF16)
    ys = [jnp.dot(dd[:, g * gd:(g + 1) * gd], w_grp_ref[g], preferred_element_type=F32)
          for g in range(ng)]
    y = jnp.concatenate(ys, axis=-1) * scale_ref[...]
    out = x + _rmsnorm(y, post_ref[...])
    y_ref[...] = out.reshape(bb, tm, d)

    for j in range(nj):
        tail = xp_ref[:, j, tm + pp - p:tm + pp, :]
        nst_ref[0, :, :, j * LANES:(j + 1) * LANES] = tail
        if carry:
            xp_ref[:, j, pp - p:pp, :] = tail


def _ffn_kernel(*refs, nc, final):
    if final:
        (x_ref, st_ref, pre_ref, w_up_ref, w_dw_ref, b_dw_ref, w_down_ref, post_ref, fin_ref,
         y_ref, nst_ref, slab_ref, cr_ref, act_ref) = refs
    else:
        (x_ref, st_ref, pre_ref, w_up_ref, w_dw_ref, b_dw_ref, w_down_ref, post_ref,
         y_ref, nst_ref, slab_ref, cr_ref, act_ref) = refs
    bb, tm, d = x_ref.shape
    f = w_down_ref.shape[0]
    p = FFN_CONV_W - 1
    pp = SUBLANES
    m = bb * tm
    njc = nc // LANES
    t = pl.program_id(1)

    @pl.when(t == 0)
    def _():
        cr_ref[...] = st_ref[0]

    x = x_ref[...].reshape(m, d)
    h = _rmsnorm(x, pre_ref[...]).astype(BF16)
    for c0 in range(0, f, nc):
        halves = []
        for hi, base in enumerate((c0, f + c0)):
            u = jnp.dot(h, w_up_ref[:, base:base + nc], preferred_element_type=F32)
            tiles = []
            for jj in range(njc):
                lanes = slice(base + jj * LANES, base + (jj + 1) * LANES)
                slot = hi * njc + jj
                u_t = u[:, jj * LANES:(jj + 1) * LANES].reshape(bb, tm, LANES)
                slab_ref[:, slot, pp - p:pp, :] = cr_ref[:, :, lanes]
                slab_ref[:, slot, pp:pp + tm, :] = u_t
                um2 = slab_ref[:, slot, pp - 2:pp - 2 + tm, :]
                um1 = slab_ref[:, slot, pp - 1:pp - 1 + tm, :]
                cr_ref[:, :, lanes] = slab_ref[:, slot, tm + pp - p:tm + pp, :]
                cc = (um2 * w_dw_ref[0:1, lanes] + um1 * w_dw_ref[1:2, lanes]
                      + u_t * w_dw_ref[2:3, lanes] + b_dw_ref[:, lanes])
                tiles.append(cc.reshape(m, LANES))
            halves.append(jnp.concatenate(tiles, axis=-1))
        gate, val = halves
        act_ref[:, c0:c0 + nc] = (gate * jax.nn.sigmoid(gate) * val).astype(BF16)

    fo = jnp.dot(act_ref[...], w_down_ref[...], preferred_element_type=F32)
    out = x + _rmsnorm(fo, post_ref[...])
    if final:
        out = _rmsnorm(out, fin_ref[...])
    y_ref[...] = out.reshape(bb, tm, d)
    nst_ref[0] = cr_ref[...]


def _const_spec(shape):
    nd = len(shape)
    return pl.BlockSpec(shape, lambda b, t: (0,) * nd, pipeline_mode=pl.Buffered(1))


def _skip_ref(kernel_fn, index):
    def wrapped(*refs):
        return kernel_fn(*refs[:index], *refs[index + 1:])
    return wrapped


_PARAMS = pltpu.CompilerParams(dimension_semantics=("arbitrary", "arbitrary"),
                               vmem_limit_bytes=VMEM_LIMIT_BYTES)


def _seq_call(kernel_fn, name, x, st_all, layer, prev_nst, consts, scratch, bb, tm):
    bsz, t_len, d = x.shape
    n_l, st_b, p, c = st_all.shape
    x_spec = pl.BlockSpec((bb, tm, d), lambda b, t: (b, t, 0))
    if st_b == bsz:
        st_spec = pl.BlockSpec((1, bb, p, c), lambda b, t: (layer, b, 0, 0))
    else:
        assert st_b == 1 and bb == 1
        st_spec = pl.BlockSpec((1, 1, p, c), lambda b, t: (layer, 0, 0, 0))
    nst_spec = pl.BlockSpec((1, bb, p, c), lambda b, t: (layer, b, 0, 0))
    inputs = [x, st_all] + list(consts)
    in_specs = [x_spec, st_spec] + [_const_spec(a.shape) for a in consts]
    aliases = {}
    if prev_nst is not None:
        aliases = {len(inputs): 1}
        kernel_fn = _skip_ref(kernel_fn, len(inputs))
        inputs.append(prev_nst)
        in_specs.append(pl.BlockSpec(memory_space=pl.ANY))
    return pl.pallas_call(
        kernel_fn,
        grid=(bsz // bb, t_len // tm),
        in_specs=in_specs,
        out_specs=[x_spec, nst_spec],
        out_shape=[jax.ShapeDtypeStruct(x.shape, F32),
                   jax.ShapeDtypeStruct((n_l, bsz, p, c), F32)],
        scratch_shapes=scratch,
        input_output_aliases=aliases,
        compiler_params=_PARAMS,
        name=name,
    )(*inputs)


def _conv_mixer(x, st_all, layer, prev_nst, w, bb, tm):
    c_dim = w["w_out"].shape[0]
    pp = _round_up(CONV_W - 1, SUBLANES)
    consts = [w["pre"], w["w_in"], w["b_in"], w["w_dw"], w["b_dw"], w["ln_g"], w["ln_b"],
              w["w_out"], w["b_out"], w["post"]]
    scratch = [pltpu.VMEM((bb, c_dim // LANES, pp + tm, LANES), F32),
               pltpu.VMEM((bb, tm, c_dim), F32),
               pltpu.VMEM((CONV_W, SUBLANES, c_dim), F32)]
    fn = functools.partial(_conv_mixer_kernel, carry=x.shape[1] > tm)
    return _seq_call(fn, "conv_mixer", x, st_all, layer, prev_nst, consts, scratch, bb, tm)


def _pool_mixer(x, st_all, layer, prev_nst, w, bb, tm, start_pos):
    d = x.shape[-1]
    pp = _round_up(max(POOL_WINDOWS) - 1, SUBLANES)
    consts = [w["pre"], w["w_grp"], w["scale"], w["post"]]
    scratch = [pltpu.VMEM((bb, d // LANES, pp + tm, LANES), F32),
               pltpu.VMEM((bb, tm, d), F32)]
    fn = functools.partial(_pool_mixer_kernel, start_pos=start_pos, carry=x.shape[1] > tm)
    return _seq_call(fn, "pool_mixer", x, st_all, layer, prev_nst, consts, scratch, bb, tm)


def _conv_ffn(x, st_all, layer, prev_nst, w, bb, tm, final_g=None):
    f = w["w_down"].shape[0]
    nc = 2 * LANES
    consts = [w["pre"], w["w_up"], w["w_dw"], w["b_dw"], w["w_down"], w["post"]]
    if final_g is not None:
        consts.append(final_g)
    scratch = [pltpu.VMEM((bb, 2 * nc // LANES, SUBLANES + tm, LANES), F32),
               pltpu.VMEM((bb, FFN_CONV_W - 1, 2 * f), F32),
               pltpu.VMEM((bb * tm, f), BF16)]
    fn = functools.partial(_ffn_kernel, nc=nc, final=final_g is not None)
    return _seq_call(fn, "conv_ffn", x, st_all, layer, prev_nst, consts, scratch, bb, tm)


def _trunk(x, st_conv, st_pool, st_ffn, start_pos, layers, final_g, bb, tm):
    new_conv = new_pool = new_ffn = None
    depth = len(layers)
    for i, (mixer, ffn) in enumerate(layers):
        if i % 2 == 0:
            x, new_conv = _conv_mixer(x, st_conv, i // 2, new_conv, mixer, bb, tm)
        else:
            x, new_pool = _pool_mixer(x, st_pool, i // 2, new_pool, mixer, bb, tm, start_pos)
        x, new_ffn = _conv_ffn(x, st_ffn, i, new_ffn, ffn, bb, tm,
                               final_g if i == depth - 1 else None)
    return x, new_conv, new_pool, new_ffn


def kernel(x_prompt, x_sample, state_conv, state_pool, state_ffn, meta_tokens, mix_pre, mix_post,
           ffn_pre, ffn_post, final_norm, a_w_in, a_b_in, a_w_dw, a_b_dw, a_ln_g, a_ln_b, a_w_out,
           a_b_out, b_w_grp, b_scale, f_w_up, f_w_dw, f_b_dw, f_w_down):
    depth = mix_pre.shape[0]
    row = lambda a: a.reshape(1, -1)
    layers = []
    for i in range(depth):
        j = i // 2
        if i % 2 == 0:
            mixer = dict(pre=row(mix_pre[i]), w_in=a_w_in[j].astype(BF16), b_in=row(a_b_in[j]),
                         w_dw=a_w_dw[j], b_dw=row(a_b_dw[j]), ln_g=row(a_ln_g[j]),
                         ln_b=row(a_ln_b[j]), w_out=a_w_out[j].astype(BF16),
                         b_out=row(a_b_out[j]), post=row(mix_post[i]))
        else:
            mixer = dict(pre=row(mix_pre[i]), w_grp=b_w_grp[j].astype(BF16),
                         scale=row(b_scale[j]), post=row(mix_post[i]))
        ffn = dict(pre=row(ffn_pre[i]), w_up=f_w_up[i].astype(BF16), w_dw=f_w_dw[i],
                   b_dw=row(f_b_dw[i]), w_down=f_w_down[i].astype(BF16), post=row(ffn_post[i]))
        layers.append((mixer, ffn))
    final_g = row(final_norm)

    zeros = lambda a: jnp.zeros((a.shape[0], 1) + a.shape[2:], F32)
    _, mc, mp, mf = _trunk(meta_tokens[None].astype(F32), zeros(state_conv), zeros(state_pool),
                           zeros(state_ffn), 0, layers, final_g, 1, N_META)
    y_prompt, pc, ppl, pf = _trunk(x_prompt, mc, mp, mf, N_META, layers, final_g, 1, 512)
    y_sample, sc, sp, sf = _trunk(x_sample, state_conv, state_pool, state_ffn, PAST_LEN, layers,
                                  final_g, 32, x_sample.shape[1])
    return (y_prompt, y_sample, pc, ppl, pf, sc, sp, sf)
```

```python
import functools

import jax
import jax.numpy as jnp
from jax import lax
from jax.experimental import pallas as pl
from jax.experimental.pallas import tpu as pltpu

LANES = 128
SUBLANES = 8
VMEM_LIMIT_BYTES = 56 * 1024 * 1024

EPS = 1e-6
N_META = 16
PAST_LEN = 16384
CONV_W = 31
POOL_WINDOWS = (2, 4, 8, 16)
FFN_CONV_W = 3
CONV_CHUNK_ROWS = 32
FFN_CHUNK = 2 * LANES

F32 = jnp.float32
BF16 = jnp.bfloat16


def _round_up(n, m):
    return (n + m - 1) // m * m


def _rmsnorm(x, g):
    ms = jnp.mean(x * x, axis=-1, keepdims=True)
    return x * lax.rsqrt(ms + EPS) * g


def _tile_dims(x_ref, bb):
    return (x_ref.shape[1] if bb == 1 else x_ref.shape[0]), x_ref.shape[2]


def _load_rows(x_ref, bb):
    if bb == 1:
        return x_ref[0]
    tm, _, d = x_ref.shape
    return x_ref[...].reshape(tm * bb, d)


def _store_rows(y_ref, val, bb):
    if bb == 1:
        y_ref[0] = val
    else:
        y_ref[...] = val.reshape(y_ref.shape)


def _hist_load(st_ref, lanes, bb):
    if bb == 1:
        return st_ref[0, 0, :, lanes]
    h = st_ref[0, :, :, lanes]
    return h.reshape(h.shape[0] * bb, h.shape[2])


def _hist_store(nst_ref, lanes, val, bb):
    if bb == 1:
        nst_ref[0, 0, :, lanes] = val
    else:
        nst_ref[0, :, :, lanes] = val.reshape(val.shape[0] // bb, bb, val.shape[1])


def _conv_mixer_kernel(x_ref, st_ref, pre_ref, w_in_ref, b_in_ref, w_dw_ref, b_dw_ref,
                       ln_g_ref, ln_b_ref, w_out_ref, b_out_ref, post_ref,
                       y_ref, nst_ref, xp_ref, c_ref, wb_ref, *, bb, carry):
    tm, d = _tile_dims(x_ref, bb)
    c_dim = w_out_ref.shape[1]
    nj = c_dim // LANES
    m = tm * bb
    hp = (CONV_W - 1) * bb
    off0 = _round_up(hp, SUBLANES)
    t = pl.program_id(1)

    @pl.when((pl.program_id(0) == 0) & (t == 0))
    def _():
        for k in range(CONV_W):
            wb_ref[k] = jnp.broadcast_to(w_dw_ref[0, k:k + 1, :], (SUBLANES, c_dim))

    @pl.when(t == 0)
    def _():
        for j in range(nj):
            xp_ref[j, off0 - hp:off0, :] = _hist_load(st_ref, slice(j * LANES, (j + 1) * LANES), bb)

    x = _load_rows(x_ref, bb)
    h = _rmsnorm(x, pre_ref[0])
    a = jnp.dot(h.astype(BF16), w_in_ref[0], preferred_element_type=F32) + b_in_ref[0]
    v = a[:, :c_dim] * jax.nn.sigmoid(a[:, c_dim:])
    for j in range(nj):
        xp_ref[j, off0:off0 + m, :] = v[:, j * LANES:(j + 1) * LANES]

    rc = min(m, CONV_CHUNK_ROWS)
    for r0 in range(0, m, rc):
        for j in range(nj):
            lanes = slice(j * LANES, (j + 1) * LANES)
            acc = None
            for k in range(CONV_W):
                lo = off0 - hp + k * bb + r0
                xs = xp_ref[j, lo:lo + rc, :].reshape(rc // SUBLANES, SUBLANES, LANES)
                term = xs * wb_ref[k, :, lanes][None]
                acc = term if acc is None else acc + term
            acc = acc + jnp.broadcast_to(b_dw_ref[0, :, lanes], (SUBLANES, LANES))[None]
            c_ref[r0:r0 + rc, lanes] = acc.reshape(rc, LANES)

    c = c_ref[...]
    mu = jnp.mean(c, axis=-1, keepdims=True)
    cc = c - mu
    var = jnp.mean(cc * cc, axis=-1, keepdims=True)
    yl = cc * lax.rsqrt(var + EPS) * ln_g_ref[0] + ln_b_ref[0]
    s = yl * jax.nn.sigmoid(yl)
    mo = jnp.dot(s.astype(BF16), w_out_ref[0], preferred_element_type=F32) + b_out_ref[0]
    _store_rows(y_ref, x + _rmsnorm(mo, post_ref[0]), bb)

    for j in range(nj):
        tail = xp_ref[j, off0 + m - hp:off0 + m, :]
        _hist_store(nst_ref, slice(j * LANES, (j + 1) * LANES), tail, bb)
        if carry:
            xp_ref[j, off0 - hp:off0, :] = tail


def _pool_mixer_kernel(x_ref, st_ref, pre_ref, w_grp_ref, scale_ref, post_ref,
                       y_ref, nst_ref, xp_ref, d_ref, *, bb, start_pos, carry):
    tm, d = _tile_dims(x_ref, bb)
    nj = d // LANES
    ng = len(POOL_WINDOWS)
    gd = d // ng
    m = tm * bb
    hp = (max(POOL_WINDOWS) - 1) * bb
    off0 = _round_up(hp, SUBLANES)
    t = pl.program_id(1)

    @pl.when(t == 0)
    def _():
        for j in range(nj):
            xp_ref[j, off0 - hp:off0, :] = _hist_load(st_ref, slice(j * LANES, (j + 1) * LANES), bb)

    x = _load_rows(x_ref, bb)
    h = _rmsnorm(x, pre_ref[0])
    for j in range(nj):
        xp_ref[j, off0:off0 + m, :] = h[:, j * LANES:(j + 1) * LANES]

    rc = min(m, CONV_CHUNK_ROWS)
    assert bb == 1 or bb % rc == 0
    for r0 in range(0, m, rc):
        if bb == 1:
            pos = start_pos + t * tm + r0 + lax.broadcasted_iota(jnp.int32, (rc, LANES), 0)
        else:
            pos = jnp.full((rc, LANES), start_pos + r0 // bb, jnp.int32) + t * tm
        for g, w in enumerate(POOL_WINDOWS):
            cnt = jnp.minimum(w, pos + 1).astype(F32)
            for j in range(g * gd // LANES, (g + 1) * gd // LANES):
                win = None
                for k in range(w):
                    lo = off0 - k * bb + r0
                    xs = xp_ref[j, lo:lo + rc, :]
                    win = xs if win is None else win + xs
                cur = xp_ref[j, off0 + r0:off0 + r0 + rc, :]
                d_ref[r0:r0 + rc, j * LANES:(j + 1) * LANES] = win / cnt - cur

    dd = d_ref[...].astype(BF16)
    ys = [jnp.dot(dd[:, g * gd:(g + 1) * gd], w_grp_ref[0, g], preferred_element_type=F32)
          for g in range(ng)]
    y = jnp.concatenate(ys, axis=-1) * scale_ref[0]
    _store_rows(y_ref, x + _rmsnorm(y, post_ref[0]), bb)

    for j in range(nj):
        tail = xp_ref[j, off0 + m - hp:off0 + m, :]
        _hist_store(nst_ref, slice(j * LANES, (j + 1) * LANES), tail, bb)
        if carry:
            xp_ref[j, off0 - hp:off0, :] = tail


def _ffn_kernel(*refs, bb, final):
    if final:
        (x_ref, st_ref, pre_ref, w_up_ref, w_dw_ref, b_dw_ref, w_down_ref, post_ref, fin_ref,
         y_ref, nst_ref, slab_ref, cr_ref, act_ref) = refs
    else:
        (x_ref, st_ref, pre_ref, w_up_ref, w_dw_ref, b_dw_ref, w_down_ref, post_ref,
         y_ref, nst_ref, slab_ref, cr_ref, act_ref) = refs
    tm, d = _tile_dims(x_ref, bb)
    f = w_down_ref.shape[1]
    nc = FFN_CHUNK
    njc = nc // LANES
    m = tm * bb
    hp = (FFN_CONV_W - 1) * bb
    off0 = _round_up(hp, SUBLANES)
    t = pl.program_id(1)

    @pl.when(t == 0)
    def _():
        cr_ref[...] = _hist_load(st_ref, slice(None), bb)

    x = _load_rows(x_ref, bb)
    h = _rmsnorm(x, pre_ref[0]).astype(BF16)
    for c0 in range(0, f, nc):
        halves = []
        for hi, base in enumerate((c0, f + c0)):
            u = jnp.dot(h, w_up_ref[0, :, base:base + nc], preferred_element_type=F32)
            tiles = []
            for jj in range(njc):
                lanes = slice(base + jj * LANES, base + (jj + 1) * LANES)
                slot = hi * njc + jj
                u_t = u[:, jj * LANES:(jj + 1) * LANES]
                slab_ref[slot, off0 - hp:off0, :] = cr_ref[:, lanes]
                slab_ref[slot, off0:off0 + m, :] = u_t
                um2 = slab_ref[slot, off0 - 2 * bb:off0 - 2 * bb + m, :]
                um1 = slab_ref[slot, off0 - bb:off0 - bb + m, :]
                cr_ref[:, lanes] = slab_ref[slot, off0 + m - hp:off0 + m, :]
                tiles.append(um2 * w_dw_ref[0, 0:1, lanes] + um1 * w_dw_ref[0, 1:2, lanes]
                             + u_t * w_dw_ref[0, 2:3, lanes] + b_dw_ref[0, :, lanes])
            halves.append(jnp.concatenate(tiles, axis=-1))
        gate, val = halves
        act_ref[:, c0:c0 + nc] = (gate * jax.nn.sigmoid(gate) * val).astype(BF16)

    fo = jnp.dot(act_ref[...], w_down_ref[0], preferred_element_type=F32)
    out = x + _rmsnorm(fo, post_ref[0])
    if final:
        out = _rmsnorm(out, fin_ref[0])
    _store_rows(y_ref, out, bb)
    _hist_store(nst_ref, slice(None), cr_ref[...], bb)


def _layer_spec(arr, layer):
    nd = arr.ndim
    return pl.BlockSpec((1,) + arr.shape[1:], lambda b, t: (layer,) + (0,) * (nd - 1),
                        pipeline_mode=pl.Buffered(1))


def _skip_ref(kernel_fn, index):
    def wrapped(*refs):
        return kernel_fn(*refs[:index], *refs[index + 1:])
    return wrapped


_PARAMS = pltpu.CompilerParams(dimension_semantics=("arbitrary", "arbitrary"),
                               vmem_limit_bytes=VMEM_LIMIT_BYTES)


def _seq_call(kernel_fn, name, x, st_all, st_layer, prev_nst, consts, scratch, bb, tm):
    if bb == 1:
        bsz, t_len, d = x.shape
        n_l, st_b, p, c = st_all.shape
        x_spec = pl.BlockSpec((1, tm, d), lambda b, t: (b, t, 0))
        if st_b == bsz:
            st_spec = pl.BlockSpec((1, 1, p, c), lambda b, t: (st_layer, b, 0, 0))
        else:
            assert st_b == 1
            st_spec = pl.BlockSpec((1, 1, p, c), lambda b, t: (st_layer, 0, 0, 0))
        nst_spec = pl.BlockSpec((1, 1, p, c), lambda b, t: (st_layer, b, 0, 0))
        nst_shape = (n_l, bsz, p, c)
    else:
        t_len, bsz, d = x.shape
        n_l, p, st_b, c = st_all.shape
        assert st_b == bsz and bb % SUBLANES == 0
        x_spec = pl.BlockSpec((tm, bb, d), lambda b, t: (t, b, 0))
        st_spec = pl.BlockSpec((1, p, bb, c), lambda b, t: (st_layer, 0, b, 0))
        nst_spec = st_spec
        nst_shape = st_all.shape
    inputs = [x, st_all] + [a for a, _ in consts]
    in_specs = [x_spec, st_spec] + [_layer_spec(a, l) for a, l in consts]
    aliases = {}
    if prev_nst is not None:
        aliases = {len(inputs): 1}
        kernel_fn = _skip_ref(kernel_fn, len(inputs))
        inputs.append(prev_nst)
        in_specs.append(pl.BlockSpec(memory_space=pl.ANY))
    return pl.pallas_call(
        kernel_fn,
        grid=(bsz // bb, t_len // tm),
        in_specs=in_specs,
        out_specs=[x_spec, nst_spec],
        out_shape=[jax.ShapeDtypeStruct(x.shape, F32), jax.ShapeDtypeStruct(nst_shape, F32)],
        scratch_shapes=scratch,
        input_output_aliases=aliases,
        compiler_params=_PARAMS,
        name=name,
    )(*inputs)


def _seq_len(x, bb):
    return x.shape[1] if bb == 1 else x.shape[0]


def _conv_mixer(x, st_all, prev_nst, w, i, bb, tm):
    j = i // 2
    c_dim = w["a_w_out"].shape[1]
    m = tm * bb
    off0 = _round_up((CONV_W - 1) * bb, SUBLANES)
    consts = [(w["mix_pre"], i), (w["a_w_in"], j), (w["a_b_in"], j), (w["a_w_dw"], j),
              (w["a_b_dw"], j), (w["a_ln_g"], j), (w["a_ln_b"], j), (w["a_w_out"], j),
              (w["a_b_out"], j), (w["mix_post"], i)]
    scratch = [pltpu.VMEM((c_dim // LANES, off0 + m, LANES), F32),
               pltpu.VMEM((m, c_dim), F32),
               pltpu.VMEM((CONV_W, SUBLANES, c_dim), F32)]
    fn = functools.partial(_conv_mixer_kernel, bb=bb, carry=_seq_len(x, bb) > tm)
    return _seq_call(fn, "conv_mixer", x, st_all, j, prev_nst, consts, scratch, bb, tm)


def _pool_mixer(x, st_all, prev_nst, w, i, bb, tm, start_pos):
    j = i // 2
    d = x.shape[-1]
    m = tm * bb
    off0 = _round_up((max(POOL_WINDOWS) - 1) * bb, SUBLANES)
    consts = [(w["mix_pre"], i), (w["b_w_grp"], j), (w["b_scale"], j), (w["mix_post"], i)]
    scratch = [pltpu.VMEM((d // LANES, off0 + m, LANES), F32),
               pltpu.VMEM((m, d), F32)]
    fn = functools.partial(_pool_mixer_kernel, bb=bb, start_pos=start_pos,
                           carry=_seq_len(x, bb) > tm)
    return _seq_call(fn, "pool_mixer", x, st_all, j, prev_nst, consts, scratch, bb, tm)


def _conv_ffn(x, st_all, prev_nst, w, i, bb, tm, final):
    f = w["f_w_down"].shape[1]
    m = tm * bb
    hp = (FFN_CONV_W - 1) * bb
    off0 = _round_up(hp, SUBLANES)
    consts = [(w["ffn_pre"], i), (w["f_w_up"], i), (w["f_w_dw"], i), (w["f_b_dw"], i),
              (w["f_w_down"], i), (w["ffn_post"], i)]
    if final:
        consts.append((w["final_norm"], 0))
    scratch = [pltpu.VMEM((2 * FFN_CHUNK // LANES, off0 + m, LANES), F32),
               pltpu.VMEM((hp, 2 * f), F32),
               pltpu.VMEM((m, f), BF16)]
    fn = functools.partial(_ffn_kernel, bb=bb, final=final)
    return _seq_call(fn, "conv_ffn", x, st_all, i, prev_nst, consts, scratch, bb, tm)


def _trunk(x, st_conv, st_pool, st_ffn, start_pos, w, bb_mix, bb_ffn, tm):
    new_conv = new_pool = new_ffn = None
    depth = w["mix_pre"].shape[0]
    for i in range(depth):
        if i % 2 == 0:
            x, new_conv = _conv_mixer(x, st_conv, new_conv, w, i, bb_mix, tm)
        else:
            x, new_pool = _pool_mixer(x, st_pool, new_pool, w, i, bb_mix, tm, start_pos)
        x, new_ffn = _conv_ffn(x, st_ffn, new_ffn, w, i, bb_ffn, tm, i == depth - 1)
    return x, new_conv, new_pool, new_ffn


def kernel(x_prompt, x_sample, state_conv, state_pool, state_ffn, meta_tokens, mix_pre, mix_post,
           ffn_pre, ffn_post, final_norm, a_w_in, a_b_in, a_w_dw, a_b_dw, a_ln_g, a_ln_b, a_w_out,
           a_b_out, b_w_grp, b_scale, f_w_up, f_w_dw, f_b_dw, f_w_down):
    rows = lambda a: a.reshape(a.shape[0], 1, a.shape[-1])
    w = dict(mix_pre=rows(mix_pre), mix_post=rows(mix_post), ffn_pre=rows(ffn_pre),
             ffn_post=rows(ffn_post), final_norm=final_norm.reshape(1, 1, -1),
             a_w_in=a_w_in.astype(BF16), a_b_in=rows(a_b_in), a_w_dw=a_w_dw, a_b_dw=rows(a_b_dw),
             a_ln_g=rows(a_ln_g), a_ln_b=rows(a_ln_b), a_w_out=a_w_out.astype(BF16),
             a_b_out=rows(a_b_out), b_w_grp=b_w_grp.astype(BF16), b_scale=rows(b_scale),
             f_w_up=f_w_up.astype(BF16), f_w_dw=f_w_dw, f_b_dw=rows(f_b_dw),
             f_w_down=f_w_down.astype(BF16))

    zeros = lambda a: jnp.zeros((a.shape[0], 1) + a.shape[2:], F32)
    _, mc, mp, mf = _trunk(meta_tokens[None].astype(F32), zeros(state_conv), zeros(state_pool),
                           zeros(state_ffn), 0, w, 1, 1, N_META)
    y_prompt, pc, ppl, pf = _trunk(x_prompt, mc, mp, mf, N_META, w, 1, 1, 512)
    tmaj = lambda a: a.transpose(0, 2, 1, 3)
    y_tm, sc, sp, sf = _trunk(x_sample.transpose(1, 0, 2), tmaj(state_conv), tmaj(state_pool),
                              tmaj(state_ffn), PAST_LEN, w, 32, 64, x_sample.shape[1])
    return (y_prompt, y_tm.transpose(1, 0, 2), pc, ppl, pf, tmaj(sc), tmaj(sp), tmaj(sf))
```

```python
import functools

import jax
import jax.numpy as jnp
from jax import lax
from jax.experimental import pallas as pl
from jax.experimental.pallas import tpu as pltpu

LANES = 128
SUBLANES = 8
MXU_WIDTH = 256
VMEM_LIMIT_BYTES = 60 * 1024 * 1024

EPS = 1e-6
N_META = 16
PAST_LEN = 16384
CONV_W = 31
POOL_WINDOWS = (2, 4, 8, 16)
POOL_P = max(POOL_WINDOWS) - 1
FFN_CONV_W = 3
CONV_CHUNK_ROWS = 32
N_CONV_CONSTS, N_POOL_CONSTS, N_FFN_CONSTS = 10, 4, 6
N_CONV_SCRATCH, N_POOL_SCRATCH = 3, 2

F32 = jnp.float32
BF16 = jnp.bfloat16


def _round_up(n, m):
    return (n + m - 1) // m * m


def _rmsnorm(x, g):
    ms = jnp.mean(x * x, axis=-1, keepdims=True)
    return x * lax.rsqrt(ms + EPS) * g


def _tile_dims(x_ref, bb):
    return (x_ref.shape[1] if bb == 1 else x_ref.shape[0]), x_ref.shape[2]


def _load_rows(x_ref, bb):
    if bb == 1:
        return x_ref[0]
    tm, _, d = x_ref.shape
    return x_ref[...].reshape(tm * bb, d)


def _store_rows(y_ref, val, bb):
    if bb == 1:
        y_ref[0] = val
    else:
        y_ref[...] = val.reshape(y_ref.shape)


def _hist_load(st_ref, lanes, bb):
    if bb == 1:
        return st_ref[0, 0, :, lanes]
    h = st_ref[0, :, :, lanes]
    return h.reshape(h.shape[0] * bb, h.shape[2])


def _hist_store(nst_ref, lanes, val, bb):
    if bb == 1:
        nst_ref[0, 0, :, lanes] = val
    else:
        nst_ref[0, :, :, lanes] = val.reshape(val.shape[0] // bb, bb, val.shape[1])


def _mixer_hist_init(st_ref, xp_ref, p, bb):
    hp = p * bb
    off0 = _round_up(hp, SUBLANES)
    for j in range(xp_ref.shape[0]):
        xp_ref[j, off0 - hp:off0, :] = _hist_load(st_ref, slice(j * LANES, (j + 1) * LANES), bb)


def _mixer_pad_init(xp_ref, p, bb, m):
    off0 = _round_up(p * bb, SUBLANES)
    if xp_ref.shape[1] > off0 + m:
        xp_ref[:, off0 + m:, :] = jnp.zeros((xp_ref.shape[0], xp_ref.shape[1] - off0 - m, LANES), F32)


def _mixer_hist_out(nst_ref, xp_ref, p, bb, m, carry):
    hp = p * bb
    off0 = _round_up(hp, SUBLANES)
    for j in range(xp_ref.shape[0]):
        tail = xp_ref[j, off0 + m - hp:off0 + m, :]
        _hist_store(nst_ref, slice(j * LANES, (j + 1) * LANES), tail, bb)
        if carry:
            xp_ref[j, off0 - hp:off0, :] = tail


def _conv_taps_init(consts, scratch):
    w_dw_ref, wb_ref = consts[3], scratch[2]
    for k in range(CONV_W):
        wb_ref[k] = jnp.broadcast_to(w_dw_ref[0, k:k + 1, :], wb_ref.shape[1:])


def _conv_mixer_pre(x, consts, scratch, bb):
    pre_ref, w_in_ref, b_in_ref = consts[:3]
    xp_ref = scratch[0]
    m = x.shape[0]
    c_dim = w_in_ref.shape[2] // 2
    off0 = _round_up((CONV_W - 1) * bb, SUBLANES)
    hb = _rmsnorm(x, pre_ref[0]).astype(BF16)
    for c0 in range(0, c_dim, MXU_WIDTH):
        cu, cg = slice(c0, c0 + MXU_WIDTH), slice(c_dim + c0, c_dim + c0 + MXU_WIDTH)
        u = jnp.dot(hb, w_in_ref[0, :, cu], preferred_element_type=F32) + b_in_ref[0, :, cu]
        gt = jnp.dot(hb, w_in_ref[0, :, cg], preferred_element_type=F32) + b_in_ref[0, :, cg]
        v = u * jax.nn.sigmoid(gt)
        for jj in range(MXU_WIDTH // LANES):
            xp_ref[c0 // LANES + jj, off0:off0 + m, :] = v[:, jj * LANES:(jj + 1) * LANES]


def _conv_slice(consts, scratch, r0, rs, bb):
    b_dw_ref = consts[4]
    xp_ref, c_ref, wb_ref = scratch
    hp = (CONV_W - 1) * bb
    lo = _round_up(hp, SUBLANES) - hp
    for j in range(xp_ref.shape[0]):
        lanes = slice(j * LANES, (j + 1) * LANES)
        acc = None
        for k in range(CONV_W):
            xs = xp_ref[j, pl.ds(r0 + (lo + k * bb), rs), :].reshape(rs // SUBLANES, SUBLANES, LANES)
            term = xs * wb_ref[k, :, lanes][None]
            acc = term if acc is None else acc + term
        acc = acc + jnp.broadcast_to(b_dw_ref[0, :, lanes], (SUBLANES, LANES))[None]
        c_ref[pl.ds(r0, rs), lanes] = acc.reshape(rs, LANES)


def _conv_mixer_post(x, consts, nst_ref, scratch, bb, carry):
    ln_g_ref, ln_b_ref, w_out_ref, b_out_ref, post_ref = consts[5:]
    xp_ref, c_ref, _ = scratch
    m = x.shape[0]
    c = c_ref[0:m, :]
    mu = jnp.mean(c, axis=-1, keepdims=True)
    cc = c - mu
    var = jnp.mean(cc * cc, axis=-1, keepdims=True)
    yl = cc * lax.rsqrt(var + EPS) * ln_g_ref[0] + ln_b_ref[0]
    s = yl * jax.nn.sigmoid(yl)
    mo = jnp.dot(s.astype(BF16), w_out_ref[0], preferred_element_type=F32) + b_out_ref[0]
    _mixer_hist_out(nst_ref, xp_ref, CONV_W - 1, bb, m, carry)
    return x + _rmsnorm(mo, post_ref[0])


def _conv_mixer_scratch(m, mp, c_dim, bb):
    off0 = _round_up((CONV_W - 1) * bb, SUBLANES)
    return [pltpu.VMEM((c_dim // LANES, off0 + mp, LANES), F32),
            pltpu.VMEM((mp, c_dim), F32),
            pltpu.VMEM((CONV_W, SUBLANES, c_dim), F32)]


def _conv_mixer_kernel(*refs, bb, carry):
    x_ref, st_ref = refs[:2]
    consts = refs[2:2 + N_CONV_CONSTS]
    y_ref, nst_ref = refs[2 + N_CONV_CONSTS:4 + N_CONV_CONSTS]
    scratch = refs[4 + N_CONV_CONSTS:]
    t = pl.program_id(1)

    @pl.when((pl.program_id(0) == 0) & (t == 0))
    def _():
        _conv_taps_init(consts, scratch)

    @pl.when(t == 0)
    def _():
        _mixer_hist_init(st_ref, scratch[0], CONV_W - 1, bb)

    x = _load_rows(x_ref, bb)
    m = x.shape[0]
    _conv_mixer_pre(x, consts, scratch, bb)
    rc = min(m, CONV_CHUNK_ROWS)
    for r0 in range(0, m, rc):
        _conv_slice(consts, scratch, r0, rc, bb)
    _store_rows(y_ref, _conv_mixer_post(x, consts, nst_ref, scratch, bb, carry), bb)


def _pool_mixer_pre(x, consts, scratch, bb):
    xp_ref = scratch[0]
    m = x.shape[0]
    off0 = _round_up(POOL_P * bb, SUBLANES)
    h = _rmsnorm(x, consts[0][0])
    for j in range(xp_ref.shape[0]):
        xp_ref[j, off0:off0 + m, :] = h[:, j * LANES:(j + 1) * LANES]


def _pool_slice(scratch, pos0, r0, rs, bb):
    xp_ref, d_ref = scratch
    nj = xp_ref.shape[0]
    off0 = _round_up(POOL_P * bb, SUBLANES)
    if bb == 1:
        pos = pos0 + r0 + lax.broadcasted_iota(jnp.int32, (rs, LANES), 0)
    else:
        assert bb % rs == 0
        pos = jnp.full((rs, LANES), r0 // bb, jnp.int32) + pos0
    for g, w in enumerate(POOL_WINDOWS):
        cnt = jnp.minimum(w, pos + 1).astype(F32)
        for j in range(g * nj // len(POOL_WINDOWS), (g + 1) * nj // len(POOL_WINDOWS)):
            win = None
            for k in range(w):
                xs = xp_ref[j, pl.ds(r0 + (off0 - k * bb), rs), :]
                win = xs if win is None else win + xs
            cur = xp_ref[j, pl.ds(r0 + off0, rs), :]
            d_ref[pl.ds(r0, rs), j * LANES:(j + 1) * LANES] = win / cnt - cur


def _pool_mixer_post(x, consts, nst_ref, scratch, bb, carry):
    _, w_grp_ref, scale_ref, post_ref = consts
    xp_ref, d_ref = scratch
    m, d = x.shape
    ng = len(POOL_WINDOWS)
    gd = d // ng
    dd = d_ref[0:m, :].astype(BF16)
    ys = [jnp.dot(dd[:, g * gd:(g + 1) * gd], w_grp_ref[0, g], preferred_element_type=F32)
          for g in range(ng)]
    y = jnp.concatenate(ys, axis=-1) * scale_ref[0]
    _mixer_hist_out(nst_ref, xp_ref, POOL_P, bb, m, carry)
    return x + _rmsnorm(y, post_ref[0])


def _pool_mixer_scratch(m, mp, d, bb):
    off0 = _round_up(POOL_P * bb, SUBLANES)
    return [pltpu.VMEM((d // LANES, off0 + mp, LANES), F32), pltpu.VMEM((mp, d), F32)]


def _pool_mixer_kernel(*refs, bb, start_pos, carry):
    x_ref, st_ref = refs[:2]
    consts = refs[2:2 + N_POOL_CONSTS]
    y_ref, nst_ref = refs[2 + N_POOL_CONSTS:4 + N_POOL_CONSTS]
    scratch = refs[4 + N_POOL_CONSTS:]
    t = pl.program_id(1)
    tm, _ = _tile_dims(x_ref, bb)

    @pl.when(t == 0)
    def _():
        _mixer_hist_init(st_ref, scratch[0], POOL_P, bb)

    x = _load_rows(x_ref, bb)
    m = x.shape[0]
    _pool_mixer_pre(x, consts, scratch, bb)
    rc = min(m, CONV_CHUNK_ROWS)
    for r0 in range(0, m, rc):
        _pool_slice(scratch, start_pos + t * tm, r0, rc, bb)
    _store_rows(y_ref, _pool_mixer_post(x, consts, nst_ref, scratch, bb, carry), bb)


def _ffn_hist_init(st_ref, cr_ref, bb):
    for c in range(cr_ref.shape[0]):
        cr_ref[c] = _hist_load(st_ref, slice(c * MXU_WIDTH, (c + 1) * MXU_WIDTH), bb)


def _ffn_hist_out(nst_ref, cr_ref, bb):
    for c in range(cr_ref.shape[0]):
        _hist_store(nst_ref, slice(c * MXU_WIDTH, (c + 1) * MXU_WIDTH), cr_ref[c], bb)


def _ffn_up(i, buf, h, consts, slab_ref, bb):
    w_up_ref, w_down_ref = consts[1], consts[4]
    n_ch = w_down_ref.shape[1]
    m = h.shape[0]
    off0 = _round_up((FFN_CONV_W - 1) * bb, SUBLANES)
    u_tiles = []
    for hi in range(2):
        u = jnp.dot(h, w_up_ref[0, i + hi * n_ch], preferred_element_type=F32)
        for jj in range(MXU_WIDTH // LANES):
            slot = hi * (MXU_WIDTH // LANES) + jj
            u_tiles.append(u[:, jj * LANES:(jj + 1) * LANES])
            slab_ref[buf, slot, off0:off0 + m, :] = u_tiles[-1]
    return u_tiles


def _ffn_gate(i, buf, consts, slab_ref, cr_ref, m, bb, u_tiles=None):
    _, _, w_dw_ref, b_dw_ref, w_down_ref = consts[:5]
    n_ch = w_down_ref.shape[1]
    hp = (FFN_CONV_W - 1) * bb
    off0 = _round_up(hp, SUBLANES)
    halves = []
    for hi in range(2):
        c = i + hi * n_ch
        wd = w_dw_ref[0, c]
        bd = b_dw_ref[0, c]
        tiles = []
        for jj in range(MXU_WIDTH // LANES):
            lanes = slice(jj * LANES, (jj + 1) * LANES)
            slot = hi * (MXU_WIDTH // LANES) + jj
            slab_ref[buf, slot, off0 - hp:off0, :] = cr_ref[c, :, lanes]
            um2 = slab_ref[buf, slot, off0 - 2 * bb:off0 - 2 * bb + m, :]
            um1 = slab_ref[buf, slot, off0 - bb:off0 - bb + m, :]
            u_t = slab_ref[buf, slot, off0:off0 + m, :] if u_tiles is None else u_tiles[slot]
            cr_ref[c, :, lanes] = slab_ref[buf, slot, off0 + m - hp:off0 + m, :]
            tiles.append(um2 * wd[0:1, lanes] + um1 * wd[1:2, lanes] + u_t * wd[2:3, lanes]
                         + bd[:, lanes])
        halves.append(jnp.concatenate(tiles, axis=-1))
    gate, val = halves
    return (gate * jax.nn.sigmoid(gate) * val).astype(BF16)


def _ffn_post(x, fo, consts):
    out = x + _rmsnorm(fo, consts[5][0])
    if len(consts) > N_FFN_CONSTS:
        out = _rmsnorm(out, consts[N_FFN_CONSTS][0])
    return out


def _ffn_state_scratch(m, n_ch, bb):
    hp = (FFN_CONV_W - 1) * bb
    return [pltpu.VMEM((2, 2 * MXU_WIDTH // LANES, _round_up(hp, SUBLANES) + m, LANES), F32),
            pltpu.VMEM((2 * n_ch, hp, MXU_WIDTH), F32)]


def _ffn_kernel(*refs, bb, n_consts):
    x_ref, st_ref = refs[:2]
    consts = refs[2:2 + n_consts]
    y_ref, nst_ref = refs[2 + n_consts:4 + n_consts]
    slab_ref, cr_ref, act_ref = refs[4 + n_consts:]
    w_down_ref = consts[4]
    _, n_ch, nc, d = w_down_ref.shape

    @pl.when(pl.program_id(1) == 0)
    def _():
        _ffn_hist_init(st_ref, cr_ref, bb)

    x = _load_rows(x_ref, bb)
    h = _rmsnorm(x, consts[0][0]).astype(BF16)
    for i in range(n_ch):
        u_tiles = _ffn_up(i, i % 2, h, consts, slab_ref, bb)
        act_ref[:, i * nc:(i + 1) * nc] = _ffn_gate(i, i % 2, consts, slab_ref, cr_ref,
                                                    x.shape[0], bb, u_tiles)
    fo = jnp.dot(act_ref[...], w_down_ref[0].reshape(n_ch * nc, d), preferred_element_type=F32)
    _store_rows(y_ref, _ffn_post(x, fo, consts), bb)
    _ffn_hist_out(nst_ref, cr_ref, bb)


def _conv_layer_kernel(*refs, n_ffn_consts, n_t, n_tiles, rs):
    n_mc, n_ms = N_CONV_CONSTS, N_CONV_SCRATCH
    x_ref, mst_ref, fst_ref = refs[:3]
    i0 = 3
    mixer_consts, i0 = refs[i0:i0 + n_mc], i0 + n_mc
    ffn_consts, i0 = refs[i0:i0 + n_ffn_consts], i0 + n_ffn_consts
    (y_ref, mnst_ref, fnst_ref), i0 = refs[i0:i0 + 3], i0 + 3
    mixer_scratch, i0 = refs[i0:i0 + n_ms], i0 + n_ms
    slab_ref, cr_ref, h_ref, acc_ref, mid_ref = refs[i0:]
    w_down_ref = ffn_consts[4]
    n_ch = w_down_ref.shape[1]
    tm = x_ref.shape[1]
    mixer_p = CONV_W - 1
    g = pl.program_id(0)
    t_mixer = lax.rem(g, n_t)
    t_ffn = lax.rem(jnp.maximum(g - 1, 0), n_t)

    @pl.when(g == 0)
    def _():
        _mixer_pad_init(mixer_scratch[0], mixer_p, 1, tm)
        _conv_taps_init(mixer_consts, mixer_scratch)

    @pl.when((g < n_tiles) & (t_mixer == 0))
    def _():
        _mixer_hist_init(mst_ref, mixer_scratch[0], mixer_p, 1)

    @pl.when((g > 0) & (t_ffn == 0))
    def _():
        _ffn_hist_init(fst_ref, cr_ref, 1)

    def run(do_ffn, do_mixer):
        prev = lax.rem(g + 1, 2)
        if do_ffn:
            h_ref[...] = _rmsnorm(mid_ref[prev], ffn_consts[0][0]).astype(BF16)
            acc_ref[...] = jnp.zeros(acc_ref.shape, F32)
        if do_mixer:
            _conv_mixer_pre(x_ref[0], mixer_consts, mixer_scratch, 1)

        def up(i, par):
            if do_ffn:
                _ffn_up(i, par, h_ref[...], ffn_consts, slab_ref, 1)

        def gate_down(i, par):
            if do_ffn:
                act = _ffn_gate(i, par, ffn_consts, slab_ref, cr_ref, tm, 1)
                acc_ref[...] += jnp.dot(act, w_down_ref[0, i], preferred_element_type=F32)

        def mixer_slice(i):
            if do_mixer:
                r0 = i * rs if isinstance(i, int) else pl.multiple_of(i * rs, SUBLANES)
                _conv_slice(mixer_consts, mixer_scratch, r0, rs, 1)

        def step(i, par):
            up(i, par)
            gate_down(i - 1, 1 - par)
            mixer_slice(i - 1)

        def body(k, carry):
            i = first + 2 * k
            step(i, first % 2)
            step(i + 1, (first + 1) % 2)
            return carry

        up(0, 0)
        first = 1 + (n_ch - 1) % 2
        if first == 2:
            step(1, 1)
        lax.fori_loop(0, (n_ch - first) // 2, body, 0)
        gate_down(n_ch - 1, (n_ch - 1) % 2)
        mixer_slice(n_ch - 1)
        if do_mixer:
            mid_ref[lax.rem(g, 2)] = _conv_mixer_post(x_ref[0], mixer_consts, mnst_ref,
                                                      mixer_scratch, 1, True)
        if do_ffn:
            y_ref[0] = _ffn_post(mid_ref[prev], acc_ref[...], ffn_consts)
            _ffn_hist_out(fnst_ref, cr_ref, 1)

    @pl.when(g == 0)
    def _():
        run(False, True)

    @pl.when((g > 0) & (g < n_tiles))
    def _():
        run(True, True)

    @pl.when(g == n_tiles)
    def _():
        run(True, False)


def _layer_spec(arr, layer, n_grid):
    idx = (layer,) + (0,) * (arr.ndim - 1)
    index_map = (lambda g: idx) if n_grid == 1 else (lambda b, t: idx)
    return pl.BlockSpec((1,) + arr.shape[1:], index_map, pipeline_mode=pl.Buffered(1))


def _skip_refs(kernel_fn, start, count):
    def wrapped(*refs):
        return kernel_fn(*refs[:start], *refs[start + count:])
    return wrapped


def _conv_consts(w, i):
    j = i // 2
    return [(w["mix_pre"], i), (w["a_w_in"], j), (w["a_b_in"], j), (w["a_w_dw"], j),
            (w["a_b_dw"], j), (w["a_ln_g"], j), (w["a_ln_b"], j), (w["a_w_out"], j),
            (w["a_b_out"], j), (w["mix_post"], i)]


def _pool_consts(w, i):
    j = i // 2
    return [(w["mix_pre"], i), (w["b_w_grp"], j), (w["b_scale"], j), (w["mix_post"], i)]


def _ffn_consts(w, i, final):
    consts = [(w["ffn_pre"], i), (w["f_w_up"], i), (w["f_w_dw"], i), (w["f_b_dw"], i),
              (w["f_w_down"], i), (w["ffn_post"], i)]
    return consts + [(w["final_norm"], 0)] if final else consts


def _seq_call(kernel_fn, name, x, st_all, st_layer, prev_nst, consts, scratch, bb, tm):
    if bb == 1:
        bsz, t_len, d = x.shape
        n_l, st_b, p, c = st_all.shape
        assert st_b in (1, bsz)
        x_spec = pl.BlockSpec((1, tm, d), lambda b, t: (b, t, 0))
        st_spec = pl.BlockSpec((1, 1, p, c), lambda b, t: (st_layer, b if st_b > 1 else 0, 0, 0))
        nst_spec = pl.BlockSpec((1, 1, p, c), lambda b, t: (st_layer, b, 0, 0))
        nst_shape = (n_l, bsz, p, c)
    else:
        t_len, bsz, d = x.shape
        n_l, p, st_b, c = st_all.shape
        assert st_b == bsz and bb % SUBLANES == 0
        x_spec = pl.BlockSpec((tm, bb, d), lambda b, t: (t, b, 0))
        st_spec = pl.BlockSpec((1, p, bb, c), lambda b, t: (st_layer, 0, b, 0))
        nst_spec = st_spec
        nst_shape = st_all.shape
    inputs = [x, st_all] + [a for a, _ in consts]
    in_specs = [x_spec, st_spec] + [_layer_spec(a, l, 2) for a, l in consts]
    aliases = {}
    if prev_nst is not None:
        aliases = {len(inputs): 1}
        kernel_fn = _skip_refs(kernel_fn, len(inputs), 1)
        inputs.append(prev_nst)
        in_specs.append(pl.BlockSpec(memory_space=pl.ANY))
    return pl.pallas_call(
        kernel_fn,
        grid=(bsz // bb, t_len // tm),
        in_specs=in_specs,
        out_specs=[x_spec, nst_spec],
        out_shape=[jax.ShapeDtypeStruct(x.shape, F32), jax.ShapeDtypeStruct(nst_shape, F32)],
        scratch_shapes=scratch,
        input_output_aliases=aliases,
        compiler_params=pltpu.CompilerParams(dimension_semantics=("arbitrary", "arbitrary"),
                                             vmem_limit_bytes=VMEM_LIMIT_BYTES),
        name=name,
    )(*inputs)


def _conv_layer_call(x, mst_all, fst_all, prev_mnst, prev_fnst, w, i, tm, final):
    bsz, t_len, d = x.shape
    n_t = t_len // tm
    n_tiles = bsz * n_t
    n_ch = w["f_w_down"].shape[1]
    rs = _round_up(pl.cdiv(tm, n_ch), SUBLANES)
    mixer_consts = _conv_consts(w, i)
    ffn_consts = _ffn_consts(w, i, final)

    def tile_spec(tile_of_step):
        def index_map(g):
            tile = tile_of_step(g)
            return (tile // n_t, lax.rem(tile, n_t), 0)
        return pl.BlockSpec((1, tm, d), index_map)

    def hist_spec(st, layer, tile_of_step, shared):
        def index_map(g):
            return (layer, 0 if shared else tile_of_step(g) // n_t, 0, 0)
        return pl.BlockSpec((1, 1) + st.shape[2:], index_map)

    mixer_tile = lambda g: jnp.minimum(g, n_tiles - 1)
    ffn_tile = lambda g: jnp.maximum(g - 1, 0)
    m_layer = i // 2
    consts = mixer_consts + ffn_consts
    inputs = [x, mst_all, fst_all] + [a for a, _ in consts]
    in_specs = ([tile_spec(mixer_tile),
                 hist_spec(mst_all, m_layer, mixer_tile, mst_all.shape[1] == 1),
                 hist_spec(fst_all, i, ffn_tile, fst_all.shape[1] == 1)]
                + [_layer_spec(a, l, 1) for a, l in consts])
    kernel_fn = functools.partial(_conv_layer_kernel, n_ffn_consts=len(ffn_consts), n_t=n_t,
                                  n_tiles=n_tiles, rs=rs)
    aliases = {}
    n_in = len(inputs)
    prevs = [(prev_mnst, 1), (prev_fnst, 2)]
    n_prev = sum(p is not None for p, _ in prevs)
    if n_prev:
        kernel_fn = _skip_refs(kernel_fn, n_in, n_prev)
        for prev, out_idx in prevs:
            if prev is not None:
                aliases[len(inputs)] = out_idx
                inputs.append(prev)
                in_specs.append(pl.BlockSpec(memory_space=pl.ANY))
    mnst_shape = (mst_all.shape[0], bsz) + mst_all.shape[2:]
    fnst_shape = (fst_all.shape[0], bsz) + fst_all.shape[2:]
    scratch = (_conv_mixer_scratch(tm, rs * n_ch, w["a_w_out"].shape[1], 1)
               + _ffn_state_scratch(tm, n_ch, 1)
               + [pltpu.VMEM((tm, d), BF16), pltpu.VMEM((tm, d), F32), pltpu.VMEM((2, tm, d), F32)])
    return pl.pallas_call(
        kernel_fn,
        grid=(n_tiles + 1,),
        in_specs=in_specs,
        out_specs=[tile_spec(ffn_tile), hist_spec(mst_all, m_layer, mixer_tile, False),
                   hist_spec(fst_all, i, ffn_tile, False)],
        out_shape=[jax.ShapeDtypeStruct(x.shape, F32), jax.ShapeDtypeStruct(mnst_shape, F32),
                   jax.ShapeDtypeStruct(fnst_shape, F32)],
        scratch_shapes=scratch,
        input_output_aliases=aliases,
        compiler_params=pltpu.CompilerParams(dimension_semantics=("arbitrary",),
                                             vmem_limit_bytes=VMEM_LIMIT_BYTES),
        name="conv_layer",
    )(*inputs)


def _seq_len(x, bb):
    return x.shape[1] if bb == 1 else x.shape[0]


def _trunk(x, st_conv, st_pool, st_ffn, start_pos, w, bb_mix, bb_ffn, tm):
    new_conv = new_pool = new_ffn = None
    depth = w["mix_pre"].shape[0]
    d = x.shape[-1]
    _, n_ch, nc, _ = w["f_w_down"].shape
    m_mix, m_ffn = tm * bb_mix, tm * bb_ffn
    carry = _seq_len(x, bb_mix) > tm
    for i in range(depth):
        final = i == depth - 1
        if i % 2 == 0 and carry and bb_mix == 1 and bb_ffn == 1:
            x, new_conv, new_ffn = _conv_layer_call(x, st_conv, st_ffn, new_conv, new_ffn, w, i,
                                                    tm, final)
            continue
        if i % 2 == 0:
            fn = functools.partial(_conv_mixer_kernel, bb=bb_mix, carry=carry)
            scratch = _conv_mixer_scratch(m_mix, m_mix, w["a_w_out"].shape[1], bb_mix)
            x, new_conv = _seq_call(fn, "conv_mixer", x, st_conv, i // 2, new_conv,
                                    _conv_consts(w, i), scratch, bb_mix, tm)
        else:
            fn = functools.partial(_pool_mixer_kernel, bb=bb_mix, start_pos=start_pos, carry=carry)
            x, new_pool = _seq_call(fn, "pool_mixer", x, st_pool, i // 2, new_pool,
                                    _pool_consts(w, i), _pool_mixer_scratch(m_mix, m_mix, d, bb_mix),
                                    bb_mix, tm)
        consts = _ffn_consts(w, i, final)
        fn = functools.partial(_ffn_kernel, bb=bb_ffn, n_consts=len(consts))
        scratch = _ffn_state_scratch(m_ffn, n_ch, bb_ffn) + [pltpu.VMEM((m_ffn, n_ch * nc), BF16)]
        x, new_ffn = _seq_call(fn, "conv_ffn", x, st_ffn, i, new_ffn, consts, scratch, bb_ffn, tm)
    return x, new_conv, new_pool, new_ffn


def kernel(x_prompt, x_sample, state_conv, state_pool, state_ffn, meta_tokens, mix_pre, mix_post,
           ffn_pre, ffn_post, final_norm, a_w_in, a_b_in, a_w_dw, a_b_dw, a_ln_g, a_ln_b, a_w_out,
           a_b_out, b_w_grp, b_scale, f_w_up, f_w_dw, f_b_dw, f_w_down):
    rows = lambda a: a.reshape(a.shape[0], 1, a.shape[-1])
    depth, d, f2 = f_w_up.shape
    nc = MXU_WIDTH
    chunked = lambda a: a.reshape(depth, -1, f2 // nc, nc).transpose(0, 2, 1, 3)
    w = dict(mix_pre=rows(mix_pre), mix_post=rows(mix_post), ffn_pre=rows(ffn_pre),
             ffn_post=rows(ffn_post), final_norm=final_norm.reshape(1, 1, -1),
             a_w_in=a_w_in.astype(BF16), a_b_in=rows(a_b_in), a_w_dw=a_w_dw, a_b_dw=rows(a_b_dw),
             a_ln_g=rows(a_ln_g), a_ln_b=rows(a_ln_b), a_w_out=a_w_out.astype(BF16),
             a_b_out=rows(a_b_out), b_w_grp=b_w_grp.astype(BF16), b_scale=rows(b_scale),
             f_w_up=chunked(f_w_up.astype(BF16)), f_w_dw=chunked(f_w_dw), f_b_dw=chunked(f_b_dw),
             f_w_down=f_w_down.astype(BF16).reshape(depth, f2 // (2 * nc), nc, d))

    zeros = lambda a: jnp.zeros((a.shape[0], 1) + a.shape[2:], F32)
    _, mc, mp, mf = _trunk(meta_tokens[None].astype(F32), zeros(state_conv), zeros(state_pool),
                           zeros(state_ffn), 0, w, 1, 1, N_META)
    y_prompt, pc, ppl, pf = _trunk(x_prompt, mc, mp, mf, N_META, w, 1, 1, 512)
    tmaj = lambda a: a.transpose(0, 2, 1, 3)
    y_tm, sc, sp, sf = _trunk(x_sample.transpose(1, 0, 2), tmaj(state_conv), tmaj(state_pool),
                              tmaj(state_ffn), PAST_LEN, w, 32, 64, x_sample.shape[1])
    return (y_prompt, y_tm.transpose(1, 0, 2), pc, ppl, pf, tmaj(sc), tmaj(sp), tmaj(sf))
```

```python
import functools

import jax
import jax.numpy as jnp
from jax import lax
from jax.experimental import pallas as pl
from jax.experimental.pallas import tpu as pltpu

LANES = 128
SUBLANES = 8
MXU_WIDTH = 2 * LANES
VMEM_LIMIT_BYTES = 60 * 1024 * 1024

EPS = 1e-6
N_META = 16
PAST_LEN = 16384
CONV_W = 31
POOL_WINDOWS = (2, 4, 8, 16)
FFN_CONV_W = 3
CONV_CHUNK_ROWS = 32

F32 = jnp.float32
BF16 = jnp.bfloat16


def _round_up(n, m):
    return (n + m - 1) // m * m


def _rmsnorm(x, g):
    ms = jnp.mean(x * x, axis=-1, keepdims=True)
    return x * lax.rsqrt(ms + EPS) * g


def _tile_dims(x_ref, bb):
    return (x_ref.shape[1] if bb == 1 else x_ref.shape[0]), x_ref.shape[2]


def _load_rows(x_ref, bb):
    if bb == 1:
        return x_ref[0]
    tm, _, d = x_ref.shape
    return x_ref[...].reshape(tm * bb, d)


def _store_rows(y_ref, val, bb):
    if bb == 1:
        y_ref[0] = val
    else:
        y_ref[...] = val.reshape(y_ref.shape)


def _hist_load(st_ref, lanes, bb):
    if bb == 1:
        return st_ref[0, 0, :, lanes]
    h = st_ref[0, :, :, lanes]
    return h.reshape(h.shape[0] * bb, h.shape[2])


def _hist_store(nst_ref, lanes, val, bb):
    if bb == 1:
        nst_ref[0, 0, :, lanes] = val
    else:
        nst_ref[0, :, :, lanes] = val.reshape(val.shape[0] // bb, bb, val.shape[1])


def _conv_mixer_kernel(x_ref, st_ref, pre_ref, w_in_ref, b_in_ref, w_dw_ref, b_dw_ref,
                       ln_g_ref, ln_b_ref, w_out_ref, b_out_ref, post_ref,
                       y_ref, nst_ref, xp_ref, c_ref, wb_ref, *, bb, carry):
    tm, d = _tile_dims(x_ref, bb)
    c_dim = w_out_ref.shape[1]
    nj = c_dim // LANES
    m = tm * bb
    hp = (CONV_W - 1) * bb
    off0 = _round_up(hp, SUBLANES)
    t = pl.program_id(1)

    @pl.when((pl.program_id(0) == 0) & (t == 0))
    def _():
        for k in range(CONV_W):
            wb_ref[k] = jnp.broadcast_to(w_dw_ref[0, k:k + 1, :], (SUBLANES, c_dim))

    @pl.when(t == 0)
    def _():
        for j in range(nj):
            xp_ref[j, off0 - hp:off0, :] = _hist_load(st_ref, slice(j * LANES, (j + 1) * LANES), bb)

    x = _load_rows(x_ref, bb)
    hb = _rmsnorm(x, pre_ref[0]).astype(BF16)
    for c0 in range(0, c_dim, MXU_WIDTH):
        cu, cg = slice(c0, c0 + MXU_WIDTH), slice(c_dim + c0, c_dim + c0 + MXU_WIDTH)
        u = jnp.dot(hb, w_in_ref[0, :, cu], preferred_element_type=F32) + b_in_ref[0, :, cu]
        gt = jnp.dot(hb, w_in_ref[0, :, cg], preferred_element_type=F32) + b_in_ref[0, :, cg]
        v = u * jax.nn.sigmoid(gt)
        for jj in range(MXU_WIDTH // LANES):
            xp_ref[c0 // LANES + jj, off0:off0 + m, :] = v[:, jj * LANES:(jj + 1) * LANES]

    rc = min(m, CONV_CHUNK_ROWS)
    for r0 in range(0, m, rc):
        for j in range(nj):
            lanes = slice(j * LANES, (j + 1) * LANES)
            acc = None
            for k in range(CONV_W):
                lo = off0 - hp + k * bb + r0
                xs = xp_ref[j, lo:lo + rc, :].reshape(rc // SUBLANES, SUBLANES, LANES)
                term = xs * wb_ref[k, :, lanes][None]
                acc = term if acc is None else acc + term
            acc = acc + jnp.broadcast_to(b_dw_ref[0, :, lanes], (SUBLANES, LANES))[None]
            c_ref[r0:r0 + rc, lanes] = acc.reshape(rc, LANES)

    c = c_ref[...]
    mu = jnp.mean(c, axis=-1, keepdims=True)
    cc = c - mu
    var = jnp.mean(cc * cc, axis=-1, keepdims=True)
    yl = cc * lax.rsqrt(var + EPS) * ln_g_ref[0] + ln_b_ref[0]
    s = yl * jax.nn.sigmoid(yl)
    mo = jnp.dot(s.astype(BF16), w_out_ref[0], preferred_element_type=F32) + b_out_ref[0]
    _store_rows(y_ref, x + _rmsnorm(mo, post_ref[0]), bb)

    for j in range(nj):
        tail = xp_ref[j, off0 + m - hp:off0 + m, :]
        _hist_store(nst_ref, slice(j * LANES, (j + 1) * LANES), tail, bb)
        if carry:
            xp_ref[j, off0 - hp:off0, :] = tail


def _pool_mixer_kernel(x_ref, st_ref, pre_ref, w_grp_ref, scale_ref, post_ref,
                       y_ref, nst_ref, xp_ref, d_ref, *, bb, start_pos, carry):
    tm, d = _tile_dims(x_ref, bb)
    nj = d // LANES
    ng = len(POOL_WINDOWS)
    gd = d // ng
    m = tm * bb
    hp = (max(POOL_WINDOWS) - 1) * bb
    off0 = _round_up(hp, SUBLANES)
    t = pl.program_id(1)

    @pl.when(t == 0)
    def _():
        for j in range(nj):
            xp_ref[j, off0 - hp:off0, :] = _hist_load(st_ref, slice(j * LANES, (j + 1) * LANES), bb)

    x = _load_rows(x_ref, bb)
    h = _rmsnorm(x, pre_ref[0])
    for j in range(nj):
        xp_ref[j, off0:off0 + m, :] = h[:, j * LANES:(j + 1) * LANES]

    rc = min(m, CONV_CHUNK_ROWS)
    assert bb == 1 or bb % rc == 0
    for r0 in range(0, m, rc):
        if bb == 1:
            pos = start_pos + t * tm + r0 + lax.broadcasted_iota(jnp.int32, (rc, LANES), 0)
        else:
            pos = jnp.full((rc, LANES), start_pos + r0 // bb, jnp.int32) + t * tm
        for g, w in enumerate(POOL_WINDOWS):
            cnt = jnp.minimum(w, pos + 1).astype(F32)
            for j in range(g * gd // LANES, (g + 1) * gd // LANES):
                win = None
                for k in range(w):
                    lo = off0 - k * bb + r0
                    xs = xp_ref[j, lo:lo + rc, :]
                    win = xs if win is None else win + xs
                cur = xp_ref[j, off0 + r0:off0 + r0 + rc, :]
                d_ref[r0:r0 + rc, j * LANES:(j + 1) * LANES] = win / cnt - cur

    dd = d_ref[...].astype(BF16)
    ys = [jnp.dot(dd[:, g * gd:(g + 1) * gd], w_grp_ref[0, g], preferred_element_type=F32)
          for g in range(ng)]
    y = jnp.concatenate(ys, axis=-1) * scale_ref[0]
    _store_rows(y_ref, x + _rmsnorm(y, post_ref[0]), bb)

    for j in range(nj):
        tail = xp_ref[j, off0 + m - hp:off0 + m, :]
        _hist_store(nst_ref, slice(j * LANES, (j + 1) * LANES), tail, bb)
        if carry:
            xp_ref[j, off0 - hp:off0, :] = tail


def _ffn_kernel(*refs, bb, final):
    if final:
        (x_ref, st_ref, pre_ref, w_up_ref, w_dw_ref, b_dw_ref, w_down_ref, post_ref, fin_ref,
         y_ref, nst_ref, slab_ref, cr_ref, act_ref) = refs
    else:
        (x_ref, st_ref, pre_ref, w_up_ref, w_dw_ref, b_dw_ref, w_down_ref, post_ref,
         y_ref, nst_ref, slab_ref, cr_ref, act_ref) = refs
    tm, d = _tile_dims(x_ref, bb)
    f = w_down_ref.shape[1]
    nc = MXU_WIDTH
    njc = nc // LANES
    m = tm * bb
    hp = (FFN_CONV_W - 1) * bb
    off0 = _round_up(hp, SUBLANES)
    t = pl.program_id(1)

    @pl.when(t == 0)
    def _():
        cr_ref[...] = _hist_load(st_ref, slice(None), bb)

    x = _load_rows(x_ref, bb)
    h = _rmsnorm(x, pre_ref[0]).astype(BF16)
    for c0 in range(0, f, nc):
        halves = []
        for hi, base in enumerate((c0, f + c0)):
            u = jnp.dot(h, w_up_ref[0, :, base:base + nc], preferred_element_type=F32)
            tiles = []
            for jj in range(njc):
                lanes = slice(base + jj * LANES, base + (jj + 1) * LANES)
                slot = hi * njc + jj
                u_t = u[:, jj * LANES:(jj + 1) * LANES]
                slab_ref[slot, off0 - hp:off0, :] = cr_ref[:, lanes]
                slab_ref[slot, off0:off0 + m, :] = u_t
                um2 = slab_ref[slot, off0 - 2 * bb:off0 - 2 * bb + m, :]
                um1 = slab_ref[slot, off0 - bb:off0 - bb + m, :]
                cr_ref[:, lanes] = slab_ref[slot, off0 + m - hp:off0 + m, :]
                tiles.append(um2 * w_dw_ref[0, 0:1, lanes] + um1 * w_dw_ref[0, 1:2, lanes]
                             + u_t * w_dw_ref[0, 2:3, lanes] + b_dw_ref[0, :, lanes])
            halves.append(jnp.concatenate(tiles, axis=-1))
        gate, val = halves
        act_ref[:, c0:c0 + nc] = (gate * jax.nn.sigmoid(gate) * val).astype(BF16)

    fo = jnp.dot(act_ref[...], w_down_ref[0], preferred_element_type=F32)
    out = x + _rmsnorm(fo, post_ref[0])
    if final:
        out = _rmsnorm(out, fin_ref[0])
    _store_rows(y_ref, out, bb)
    _hist_store(nst_ref, slice(None), cr_ref[...], bb)


def _layer_spec(arr, layer):
    nd = arr.ndim
    return pl.BlockSpec((1,) + arr.shape[1:], lambda b, t: (layer,) + (0,) * (nd - 1),
                        pipeline_mode=pl.Buffered(1))


def _skip_ref(kernel_fn, index):
    def wrapped(*refs):
        return kernel_fn(*refs[:index], *refs[index + 1:])
    return wrapped


_PARAMS = pltpu.CompilerParams(dimension_semantics=("arbitrary", "arbitrary"),
                               vmem_limit_bytes=VMEM_LIMIT_BYTES)


def _seq_call(kernel_fn, name, x, st_all, st_layer, prev_nst, consts, scratch, bb, tm):
    if bb == 1:
        bsz, t_len, d = x.shape
        n_l, st_b, p, c = st_all.shape
        x_spec = pl.BlockSpec((1, tm, d), lambda b, t: (b, t, 0))
        if st_b == bsz:
            st_spec = pl.BlockSpec((1, 1, p, c), lambda b, t: (st_layer, b, 0, 0))
        else:
            assert st_b == 1
            st_spec = pl.BlockSpec((1, 1, p, c), lambda b, t: (st_layer, 0, 0, 0))
        nst_spec = pl.BlockSpec((1, 1, p, c), lambda b, t: (st_layer, b, 0, 0))
        nst_shape = (n_l, bsz, p, c)
    else:
        t_len, bsz, d = x.shape
        n_l, p, st_b, c = st_all.shape
        assert st_b == bsz and bb % SUBLANES == 0
        x_spec = pl.BlockSpec((tm, bb, d), lambda b, t: (t, b, 0))
        st_spec = pl.BlockSpec((1, p, bb, c), lambda b, t: (st_layer, 0, b, 0))
        nst_spec = st_spec
        nst_shape = st_all.shape
    inputs = [x, st_all] + [a for a, _ in consts]
    in_specs = [x_spec, st_spec] + [_layer_spec(a, l) for a, l in consts]
    aliases = {}
    if prev_nst is not None:
        aliases = {len(inputs): 1}
        kernel_fn = _skip_ref(kernel_fn, len(inputs))
        inputs.append(prev_nst)
        in_specs.append(pl.BlockSpec(memory_space=pl.ANY))
    return pl.pallas_call(
        kernel_fn,
        grid=(bsz // bb, t_len // tm),
        in_specs=in_specs,
        out_specs=[x_spec, nst_spec],
        out_shape=[jax.ShapeDtypeStruct(x.shape, F32), jax.ShapeDtypeStruct(nst_shape, F32)],
        scratch_shapes=scratch,
        input_output_aliases=aliases,
        compiler_params=_PARAMS,
        name=name,
    )(*inputs)


def _seq_len(x, bb):
    return x.shape[1] if bb == 1 else x.shape[0]


def _conv_mixer(x, st_all, prev_nst, w, i, bb, tm):
    j = i // 2
    c_dim = w["a_w_out"].shape[1]
    m = tm * bb
    off0 = _round_up((CONV_W - 1) * bb, SUBLANES)
    consts = [(w["mix_pre"], i), (w["a_w_in"], j), (w["a_b_in"], j), (w["a_w_dw"], j),
              (w["a_b_dw"], j), (w["a_ln_g"], j), (w["a_ln_b"], j), (w["a_w_out"], j),
              (w["a_b_out"], j), (w["mix_post"], i)]
    scratch = [pltpu.VMEM((c_dim // LANES, off0 + m, LANES), F32),
               pltpu.VMEM((m, c_dim), F32),
               pltpu.VMEM((CONV_W, SUBLANES, c_dim), F32)]
    fn = functools.partial(_conv_mixer_kernel, bb=bb, carry=_seq_len(x, bb) > tm)
    return _seq_call(fn, "conv_mixer", x, st_all, j, prev_nst, consts, scratch, bb, tm)


def _pool_mixer(x, st_all, prev_nst, w, i, bb, tm, start_pos):
    j = i // 2
    d = x.shape[-1]
    m = tm * bb
    off0 = _round_up((max(POOL_WINDOWS) - 1) * bb, SUBLANES)
    consts = [(w["mix_pre"], i), (w["b_w_grp"], j), (w["b_scale"], j), (w["mix_post"], i)]
    scratch = [pltpu.VMEM((d // LANES, off0 + m, LANES), F32),
               pltpu.VMEM((m, d), F32)]
    fn = functools.partial(_pool_mixer_kernel, bb=bb, start_pos=start_pos,
                           carry=_seq_len(x, bb) > tm)
    return _seq_call(fn, "pool_mixer", x, st_all, j, prev_nst, consts, scratch, bb, tm)


def _conv_ffn(x, st_all, prev_nst, w, i, bb, tm, final):
    f = w["f_w_down"].shape[1]
    m = tm * bb
    hp = (FFN_CONV_W - 1) * bb
    off0 = _round_up(hp, SUBLANES)
    consts = [(w["ffn_pre"], i), (w["f_w_up"], i), (w["f_w_dw"], i), (w["f_b_dw"], i),
              (w["f_w_down"], i), (w["ffn_post"], i)]
    if final:
        consts.append((w["final_norm"], 0))
    scratch = [pltpu.VMEM((2 * MXU_WIDTH // LANES, off0 + m, LANES), F32),
               pltpu.VMEM((hp, 2 * f), F32),
               pltpu.VMEM((m, f), BF16)]
    fn = functools.partial(_ffn_kernel, bb=bb, final=final)
    return _seq_call(fn, "conv_ffn", x, st_all, i, prev_nst, consts, scratch, bb, tm)


def _trunk(x, st_conv, st_pool, st_ffn, start_pos, w, tiles):
    new_conv = new_pool = new_ffn = None
    depth = w["mix_pre"].shape[0]
    for i in range(depth):
        if i % 2 == 0:
            x, new_conv = _conv_mixer(x, st_conv, new_conv, w, i, *tiles["conv"])
        else:
            x, new_pool = _pool_mixer(x, st_pool, new_pool, w, i, *tiles["pool"], start_pos)
        x, new_ffn = _conv_ffn(x, st_ffn, new_ffn, w, i, *tiles["ffn"], i == depth - 1)
    return x, new_conv, new_pool, new_ffn


def kernel(x_prompt, x_sample, state_conv, state_pool, state_ffn, meta_tokens, mix_pre, mix_post,
           ffn_pre, ffn_post, final_norm, a_w_in, a_b_in, a_w_dw, a_b_dw, a_ln_g, a_ln_b, a_w_out,
           a_b_out, b_w_grp, b_scale, f_w_up, f_w_dw, f_b_dw, f_w_down):
    rows = lambda a: a.reshape(a.shape[0], 1, a.shape[-1])
    w = dict(mix_pre=rows(mix_pre), mix_post=rows(mix_post), ffn_pre=rows(ffn_pre),
             ffn_post=rows(ffn_post), final_norm=final_norm.reshape(1, 1, -1),
             a_w_in=a_w_in.astype(BF16), a_b_in=rows(a_b_in), a_w_dw=a_w_dw, a_b_dw=rows(a_b_dw),
             a_ln_g=rows(a_ln_g), a_ln_b=rows(a_ln_b), a_w_out=a_w_out.astype(BF16),
             a_b_out=rows(a_b_out), b_w_grp=b_w_grp.astype(BF16), b_scale=rows(b_scale),
             f_w_up=f_w_up.astype(BF16), f_w_dw=f_w_dw, f_b_dw=rows(f_b_dw),
             f_w_down=f_w_down.astype(BF16))
    t_dec = x_sample.shape[1]
    meta_tiles = dict(conv=(1, N_META), pool=(1, N_META), ffn=(1, N_META))
    prompt_tiles = dict(conv=(1, 512), pool=(1, 1024), ffn=(1, 1024))
    sample_tiles = dict(conv=(32, t_dec), pool=(64, t_dec), ffn=(64, t_dec))

    zeros = lambda a: jnp.zeros((a.shape[0], 1) + a.shape[2:], F32)
    _, mc, mp, mf = _trunk(meta_tokens[None].astype(F32), zeros(state_conv), zeros(state_pool),
                           zeros(state_ffn), 0, w, meta_tiles)
    y_prompt, pc, ppl, pf = _trunk(x_prompt, mc, mp, mf, N_META, w, prompt_tiles)
    tmaj = lambda a: a.transpose(0, 2, 1, 3)
    y_tm, sc, sp, sf = _trunk(x_sample.transpose(1, 0, 2), tmaj(state_conv), tmaj(state_pool),
                              tmaj(state_ffn), PAST_LEN, w, sample_tiles)
    return (y_prompt, y_tm.transpose(1, 0, 2), pc, ppl, pf, tmaj(sc), tmaj(sp), tmaj(sf))
```

```python
import functools

import jax
import jax.numpy as jnp
from jax import lax
from jax.experimental import pallas as pl
from jax.experimental.pallas import tpu as pltpu

LANES = 128
SUBLANES = 8
MXU_WIDTH = 2 * LANES
VMEM_LIMIT_BYTES = 60 * 1024 * 1024

EPS = 1e-6
N_META = 16
PAST_LEN = 16384
CONV_W = 31
POOL_WINDOWS = (2, 4, 8, 16)
FFN_CONV_W = 3
CONV_CHUNK_ROWS = 32

F32 = jnp.float32
BF16 = jnp.bfloat16


def _round_up(n, m):
    return (n + m - 1) // m * m


def _rmsnorm(x, g):
    ms = jnp.mean(x * x, axis=-1, keepdims=True)
    return x * lax.rsqrt(ms + EPS) * g


def _tile_dims(x_ref, bb):
    return (x_ref.shape[1] if bb == 1 else x_ref.shape[0]), x_ref.shape[2]


def _load_rows(x_ref, bb):
    if bb == 1:
        return x_ref[0]
    tm, _, d = x_ref.shape
    return x_ref[...].reshape(tm * bb, d)


def _store_rows(y_ref, val, bb):
    if bb == 1:
        y_ref[0] = val
    else:
        y_ref[...] = val.reshape(y_ref.shape)


def _hist_load(st_ref, lanes, bb):
    if bb == 1:
        return st_ref[0, 0, :, lanes]
    h = st_ref[0, :, :, lanes]
    return h.reshape(h.shape[0] * bb, h.shape[2])


def _hist_store(nst_ref, lanes, val, bb):
    if bb == 1:
        nst_ref[0, 0, :, lanes] = val
    else:
        nst_ref[0, :, :, lanes] = val.reshape(val.shape[0] // bb, bb, val.shape[1])


def _conv_mixer_kernel(x_ref, st_ref, pre_ref, w_in_ref, b_in_ref, w_dw_ref, b_dw_ref,
                       ln_g_ref, ln_b_ref, w_out_ref, b_out_ref, post_ref,
                       y_ref, nst_ref, xp_ref, c_ref, wb_ref, *, bb, carry):
    tm, d = _tile_dims(x_ref, bb)
    c_dim = w_out_ref.shape[1]
    nj = c_dim // LANES
    m = tm * bb
    hp = (CONV_W - 1) * bb
    off0 = _round_up(hp, SUBLANES)
    t = pl.program_id(1)

    @pl.when((pl.program_id(0) == 0) & (t == 0))
    def _():
        for k in range(CONV_W):
            wb_ref[k] = jnp.broadcast_to(w_dw_ref[0, k:k + 1, :], (SUBLANES, c_dim))

    @pl.when(t == 0)
    def _():
        for j in range(nj):
            xp_ref[j, off0 - hp:off0, :] = _hist_load(st_ref, slice(j * LANES, (j + 1) * LANES), bb)

    x = _load_rows(x_ref, bb)
    hb = _rmsnorm(x, pre_ref[0]).astype(BF16)
    rc = min(m, CONV_CHUNK_ROWS)

    def taps(j, r0):
        lanes = slice(j * LANES, (j + 1) * LANES)
        acc = None
        for k in range(CONV_W):
            xs = xp_ref[j, pl.ds(r0 + (off0 - hp + k * bb), rc), :]
            term = xs.reshape(rc // SUBLANES, SUBLANES, LANES) * wb_ref[k, :, lanes][None]
            acc = term if acc is None else acc + term
        acc = acc + jnp.broadcast_to(b_dw_ref[0, :, lanes], (SUBLANES, LANES))[None]
        c_ref[pl.ds(r0, rc), lanes] = acc.reshape(rc, LANES)

    for c0 in range(0, c_dim, MXU_WIDTH):
        cu, cg = slice(c0, c0 + MXU_WIDTH), slice(c_dim + c0, c_dim + c0 + MXU_WIDTH)
        u = jnp.dot(hb, w_in_ref[0, :, cu], preferred_element_type=F32) + b_in_ref[0, :, cu]
        gt = jnp.dot(hb, w_in_ref[0, :, cg], preferred_element_type=F32) + b_in_ref[0, :, cg]
        v = u * jax.nn.sigmoid(gt)
        for jj in range(MXU_WIDTH // LANES):
            xp_ref[c0 // LANES + jj, off0:off0 + m, :] = v[:, jj * LANES:(jj + 1) * LANES]
        if bb == 1:
            for j in range(c0 // LANES, (c0 + MXU_WIDTH) // LANES):
                for r0 in range(0, m, rc):
                    taps(j, r0)

    if bb > 1:
        def chunk(i, carry):
            for j in range(nj):
                taps(j, pl.multiple_of(i * rc, rc))
            return carry

        lax.fori_loop(0, m // rc, chunk, 0)

    c = c_ref[...]
    mu = jnp.mean(c, axis=-1, keepdims=True)
    cc = c - mu
    var = jnp.mean(cc * cc, axis=-1, keepdims=True)
    yl = cc * lax.rsqrt(var + EPS) * ln_g_ref[0] + ln_b_ref[0]
    s = yl * jax.nn.sigmoid(yl)
    mo = jnp.dot(s.astype(BF16), w_out_ref[0], preferred_element_type=F32) + b_out_ref[0]
    _store_rows(y_ref, x + _rmsnorm(mo, post_ref[0]), bb)

    for j in range(nj):
        tail = xp_ref[j, off0 + m - hp:off0 + m, :]
        _hist_store(nst_ref, slice(j * LANES, (j + 1) * LANES), tail, bb)
        if carry:
            xp_ref[j, off0 - hp:off0, :] = tail


def _pool_mixer_kernel(x_ref, st_ref, pre_ref, w_grp_ref, scale_ref, post_ref,
                       y_ref, nst_ref, xp_ref, d_ref, *, bb, start_pos, carry):
    tm, d = _tile_dims(x_ref, bb)
    nj = d // LANES
    ng = len(POOL_WINDOWS)
    gd = d // ng
    m = tm * bb
    hp = (max(POOL_WINDOWS) - 1) * bb
    off0 = _round_up(hp, SUBLANES)
    t = pl.program_id(1)

    @pl.when(t == 0)
    def _():
        for j in range(nj):
            xp_ref[j, off0 - hp:off0, :] = _hist_load(st_ref, slice(j * LANES, (j + 1) * LANES), bb)

    x = _load_rows(x_ref, bb)
    h = _rmsnorm(x, pre_ref[0])
    for j in range(nj):
        xp_ref[j, off0:off0 + m, :] = h[:, j * LANES:(j + 1) * LANES]

    rc = min(m, CONV_CHUNK_ROWS)
    assert bb == 1 or bb % rc == 0
    for r0 in range(0, m, rc):
        if bb == 1:
            pos = start_pos + t * tm + r0 + lax.broadcasted_iota(jnp.int32, (rc, LANES), 0)
        else:
            pos = jnp.full((rc, LANES), start_pos + r0 // bb, jnp.int32) + t * tm
        for g, w in enumerate(POOL_WINDOWS):
            cnt = jnp.minimum(w, pos + 1).astype(F32)
            for j in range(g * gd // LANES, (g + 1) * gd // LANES):
                win = None
                for k in range(w):
                    lo = off0 - k * bb + r0
                    xs = xp_ref[j, lo:lo + rc, :]
                    win = xs if win is None else win + xs
                cur = xp_ref[j, off0 + r0:off0 + r0 + rc, :]
                d_ref[r0:r0 + rc, j * LANES:(j + 1) * LANES] = win / cnt - cur

    dd = d_ref[...].astype(BF16)
    ys = [jnp.dot(dd[:, g * gd:(g + 1) * gd], w_grp_ref[0, g], preferred_element_type=F32)
          for g in range(ng)]
    y = jnp.concatenate(ys, axis=-1) * scale_ref[0]
    _store_rows(y_ref, x + _rmsnorm(y, post_ref[0]), bb)

    for j in range(nj):
        tail = xp_ref[j, off0 + m - hp:off0 + m, :]
        _hist_store(nst_ref, slice(j * LANES, (j + 1) * LANES), tail, bb)
        if carry:
            xp_ref[j, off0 - hp:off0, :] = tail


def _ffn_kernel(*refs, bb, final):
    if final:
        (x_ref, st_ref, pre_ref, w_up_ref, w_dw_ref, b_dw_ref, w_down_ref, post_ref, fin_ref,
         y_ref, nst_ref, slab_ref, cr_ref, act_ref) = refs
    else:
        (x_ref, st_ref, pre_ref, w_up_ref, w_dw_ref, b_dw_ref, w_down_ref, post_ref,
         y_ref, nst_ref, slab_ref, cr_ref, act_ref) = refs
    tm, d = _tile_dims(x_ref, bb)
    f = w_down_ref.shape[1]
    nc = MXU_WIDTH
    njc = nc // LANES
    m = tm * bb
    hp = (FFN_CONV_W - 1) * bb
    off0 = _round_up(hp, SUBLANES)
    t = pl.program_id(1)

    @pl.when(t == 0)
    def _():
        cr_ref[...] = _hist_load(st_ref, slice(None), bb)

    x = _load_rows(x_ref, bb)
    h = _rmsnorm(x, pre_ref[0]).astype(BF16)
    for c0 in range(0, f, nc):
        halves = []
        for hi, base in enumerate((c0, f + c0)):
            u = jnp.dot(h, w_up_ref[0, :, base:base + nc], preferred_element_type=F32)
            tiles = []
            for jj in range(njc):
                lanes = slice(base + jj * LANES, base + (jj + 1) * LANES)
                slot = hi * njc + jj
                u_t = u[:, jj * LANES:(jj + 1) * LANES]
                slab_ref[slot, off0 - hp:off0, :] = cr_ref[:, lanes]
                slab_ref[slot, off0:off0 + m, :] = u_t
                um2 = slab_ref[slot, off0 - 2 * bb:off0 - 2 * bb + m, :]
                um1 = slab_ref[slot, off0 - bb:off0 - bb + m, :]
                cr_ref[:, lanes] = slab_ref[slot, off0 + m - hp:off0 + m, :]
                tiles.append(um2 * w_dw_ref[0, 0:1, lanes] + um1 * w_dw_ref[0, 1:2, lanes]
                             + u_t * w_dw_ref[0, 2:3, lanes] + b_dw_ref[0, :, lanes])
            halves.append(jnp.concatenate(tiles, axis=-1))
        gate, val = halves
        act_ref[:, c0:c0 + nc] = (gate * jax.nn.sigmoid(gate) * val).astype(BF16)

    fo = jnp.dot(act_ref[...], w_down_ref[0], preferred_element_type=F32)
    out = x + _rmsnorm(fo, post_ref[0])
    if final:
        out = _rmsnorm(out, fin_ref[0])
    _store_rows(y_ref, out, bb)
    _hist_store(nst_ref, slice(None), cr_ref[...], bb)


def _layer_spec(arr, layer):
    nd = arr.ndim
    return pl.BlockSpec((1,) + arr.shape[1:], lambda b, t: (layer,) + (0,) * (nd - 1),
                        pipeline_mode=pl.Buffered(1))


def _skip_ref(kernel_fn, index):
    def wrapped(*refs):
        return kernel_fn(*refs[:index], *refs[index + 1:])
    return wrapped


_PARAMS = pltpu.CompilerParams(dimension_semantics=("arbitrary", "arbitrary"),
                               vmem_limit_bytes=VMEM_LIMIT_BYTES)


def _seq_call(kernel_fn, name, x, st_all, st_layer, prev_nst, consts, scratch, bb, tm):
    if bb == 1:
        bsz, t_len, d = x.shape
        n_l, st_b, p, c = st_all.shape
        x_spec = pl.BlockSpec((1, tm, d), lambda b, t: (b, t, 0))
        if st_b == bsz:
            st_spec = pl.BlockSpec((1, 1, p, c), lambda b, t: (st_layer, b, 0, 0))
        else:
            assert st_b == 1
            st_spec = pl.BlockSpec((1, 1, p, c), lambda b, t: (st_layer, 0, 0, 0))
        nst_spec = pl.BlockSpec((1, 1, p, c), lambda b, t: (st_layer, b, 0, 0))
        nst_shape = (n_l, bsz, p, c)
    else:
        t_len, bsz, d = x.shape
        n_l, p, st_b, c = st_all.shape
        assert st_b == bsz and bb % SUBLANES == 0
        x_spec = pl.BlockSpec((tm, bb, d), lambda b, t: (t, b, 0))
        st_spec = pl.BlockSpec((1, p, bb, c), lambda b, t: (st_layer, 0, b, 0))
        nst_spec = st_spec
        nst_shape = st_all.shape
    inputs = [x, st_all] + [a for a, _ in consts]
    in_specs = [x_spec, st_spec] + [_layer_spec(a, l) for a, l in consts]
    aliases = {}
    if prev_nst is not None:
        aliases = {len(inputs): 1}
        kernel_fn = _skip_ref(kernel_fn, len(inputs))
        inputs.append(prev_nst)
        in_specs.append(pl.BlockSpec(memory_space=pl.ANY))
    return pl.pallas_call(
        kernel_fn,
        grid=(bsz // bb, t_len // tm),
        in_specs=in_specs,
        out_specs=[x_spec, nst_spec],
        out_shape=[jax.ShapeDtypeStruct(x.shape, F32), jax.ShapeDtypeStruct(nst_shape, F32)],
        scratch_shapes=scratch,
        input_output_aliases=aliases,
        compiler_params=_PARAMS,
        name=name,
    )(*inputs)


def _seq_len(x, bb):
    return x.shape[1] if bb == 1 else x.shape[0]


def _conv_mixer(x, st_all, prev_nst, w, i, bb, tm):
    j = i // 2
    c_dim = w["a_w_out"].shape[1]
    m = tm * bb
    off0 = _round_up((CONV_W - 1) * bb, SUBLANES)
    consts = [(w["mix_pre"], i), (w["a_w_in"], j), (w["a_b_in"], j), (w["a_w_dw"], j),
              (w["a_b_dw"], j), (w["a_ln_g"], j), (w["a_ln_b"], j), (w["a_w_out"], j),
              (w["a_b_out"], j), (w["mix_post"], i)]
    scratch = [pltpu.VMEM((c_dim // LANES, off0 + m, LANES), F32),
               pltpu.VMEM((m, c_dim), F32),
               pltpu.VMEM((CONV_W, SUBLANES, c_dim), F32)]
    fn = functools.partial(_conv_mixer_kernel, bb=bb, carry=_seq_len(x, bb) > tm)
    return _seq_call(fn, "conv_mixer", x, st_all, j, prev_nst, consts, scratch, bb, tm)


def _pool_mixer(x, st_all, prev_nst, w, i, bb, tm, start_pos):
    j = i // 2
    d = x.shape[-1]
    m = tm * bb
    off0 = _round_up((max(POOL_WINDOWS) - 1) * bb, SUBLANES)
    consts = [(w["mix_pre"], i), (w["b_w_grp"], j), (w["b_scale"], j), (w["mix_post"], i)]
    scratch = [pltpu.VMEM((d // LANES, off0 + m, LANES), F32),
               pltpu.VMEM((m, d), F32)]
    fn = functools.partial(_pool_mixer_kernel, bb=bb, start_pos=start_pos,
                           carry=_seq_len(x, bb) > tm)
    return _seq_call(fn, "pool_mixer", x, st_all, j, prev_nst, consts, scratch, bb, tm)


def _conv_ffn(x, st_all, prev_nst, w, i, bb, tm, final):
    f = w["f_w_down"].shape[1]
    m = tm * bb
    hp = (FFN_CONV_W - 1) * bb
    off0 = _round_up(hp, SUBLANES)
    consts = [(w["ffn_pre"], i), (w["f_w_up"], i), (w["f_w_dw"], i), (w["f_b_dw"], i),
              (w["f_w_down"], i), (w["ffn_post"], i)]
    if final:
        consts.append((w["final_norm"], 0))
    scratch = [pltpu.VMEM((2 * MXU_WIDTH // LANES, off0 + m, LANES), F32),
               pltpu.VMEM((hp, 2 * f), F32),
               pltpu.VMEM((m, f), BF16)]
    fn = functools.partial(_ffn_kernel, bb=bb, final=final)
    return _seq_call(fn, "conv_ffn", x, st_all, i, prev_nst, consts, scratch, bb, tm)


def _trunk(x, st_conv, st_pool, st_ffn, start_pos, w, tiles):
    new_conv = new_pool = new_ffn = None
    depth = w["mix_pre"].shape[0]
    for i in range(depth):
        if i % 2 == 0:
            x, new_conv = _conv_mixer(x, st_conv, new_conv, w, i, *tiles["conv"])
        else:
            x, new_pool = _pool_mixer(x, st_pool, new_pool, w, i, *tiles["pool"], start_pos)
        x, new_ffn = _conv_ffn(x, st_ffn, new_ffn, w, i, *tiles["ffn"], i == depth - 1)
    return x, new_conv, new_pool, new_ffn


def kernel(x_prompt, x_sample, state_conv, state_pool, state_ffn, meta_tokens, mix_pre, mix_post,
           ffn_pre, ffn_post, final_norm, a_w_in, a_b_in, a_w_dw, a_b_dw, a_ln_g, a_ln_b, a_w_out,
           a_b_out, b_w_grp, b_scale, f_w_up, f_w_dw, f_b_dw, f_w_down):
    rows = lambda a: a.reshape(a.shape[0], 1, a.shape[-1])
    w = dict(mix_pre=rows(mix_pre), mix_post=rows(mix_post), ffn_pre=rows(ffn_pre),
             ffn_post=rows(ffn_post), final_norm=final_norm.reshape(1, 1, -1),
             a_w_in=a_w_in.astype(BF16), a_b_in=rows(a_b_in), a_w_dw=a_w_dw, a_b_dw=rows(a_b_dw),
             a_ln_g=rows(a_ln_g), a_ln_b=rows(a_ln_b), a_w_out=a_w_out.astype(BF16),
             a_b_out=rows(a_b_out), b_w_grp=b_w_grp.astype(BF16), b_scale=rows(b_scale),
             f_w_up=f_w_up.astype(BF16), f_w_dw=f_w_dw, f_b_dw=rows(f_b_dw),
             f_w_down=f_w_down.astype(BF16))
    t_dec = x_sample.shape[1]
    meta_tiles = dict(conv=(1, N_META), pool=(1, N_META), ffn=(1, N_META))
    prompt_tiles = dict(conv=(1, 1024), pool=(1, 1024), ffn=(1, 1024))
    sample_tiles = dict(conv=(32, t_dec), pool=(64, t_dec), ffn=(64, t_dec))

    zeros = lambda a: jnp.zeros((a.shape[0], 1) + a.shape[2:], F32)
    _, mc, mp, mf = _trunk(meta_tokens[None].astype(F32), zeros(state_conv), zeros(state_pool),
                           zeros(state_ffn), 0, w, meta_tiles)
    y_prompt, pc, ppl, pf = _trunk(x_prompt, mc, mp, mf, N_META, w, prompt_tiles)
    tmaj = lambda a: a.transpose(0, 2, 1, 3)
    y_tm, sc, sp, sf = _trunk(x_sample.transpose(1, 0, 2), tmaj(state_conv), tmaj(state_pool),
                              tmaj(state_ffn), PAST_LEN, w, sample_tiles)
    return (y_prompt, y_tm.transpose(1, 0, 2), pc, ppl, pf, tmaj(sc), tmaj(sp), tmaj(sf))
```

```python
import functools

import jax
import jax.numpy as jnp
from jax import lax
from jax.experimental import pallas as pl
from jax.experimental.pallas import tpu as pltpu

LANES = 128
SUBLANES = 8
MXU_WIDTH = 2 * LANES
VMEM_LIMIT_BYTES = 60 * 1024 * 1024

EPS = 1e-6
N_META = 16
PAST_LEN = 16384
CONV_W = 31
POOL_WINDOWS = (2, 4, 8, 16)
FFN_CONV_W = 3
CONV_CHUNK_ROWS = 32

F32 = jnp.float32
BF16 = jnp.bfloat16


def _round_up(n, m):
    return (n + m - 1) // m * m


def _rmsnorm(x, g):
    ms = jnp.mean(x * x, axis=-1, keepdims=True)
    return x * lax.rsqrt(ms + EPS) * g


def _tile_dims(x_ref, bb):
    return (x_ref.shape[1] if bb == 1 else x_ref.shape[0]), x_ref.shape[2]


def _load_rows(x_ref, bb):
    if bb == 1:
        return x_ref[0]
    tm, _, d = x_ref.shape
    return x_ref[...].reshape(tm * bb, d)


def _store_rows(y_ref, val, bb):
    if bb == 1:
        y_ref[0] = val
    else:
        y_ref[...] = val.reshape(y_ref.shape)


def _hist_load(st_ref, lanes, bb):
    if bb == 1:
        return st_ref[0, 0, :, lanes]
    h = st_ref[0, :, :, lanes]
    return h.reshape(h.shape[0] * bb, h.shape[2])


def _hist_store(nst_ref, lanes, val, bb):
    if bb == 1:
        nst_ref[0, 0, :, lanes] = val
    else:
        nst_ref[0, :, :, lanes] = val.reshape(val.shape[0] // bb, bb, val.shape[1])


def _conv_mixer_kernel(x_ref, st_ref, pre_ref, w_in_ref, b_in_ref, w_dw_ref, b_dw_ref,
                       ln_g_ref, ln_b_ref, w_out_ref, b_out_ref, post_ref,
                       y_ref, nst_ref, xp_ref, c_ref, wb_ref, *, bb, carry):
    tm, d = _tile_dims(x_ref, bb)
    c_dim = w_out_ref.shape[1]
    nj = c_dim // LANES
    m = tm * bb
    hp = (CONV_W - 1) * bb
    off0 = _round_up(hp, SUBLANES)
    t = pl.program_id(1)

    @pl.when((pl.program_id(0) == 0) & (t == 0))
    def _():
        for k in range(CONV_W):
            wb_ref[k] = jnp.broadcast_to(w_dw_ref[0, k:k + 1, :], (SUBLANES, c_dim))

    @pl.when(t == 0)
    def _():
        for j in range(nj):
            xp_ref[j, off0 - hp:off0, :] = _hist_load(st_ref, slice(j * LANES, (j + 1) * LANES), bb)

    x = _load_rows(x_ref, bb)
    hb = _rmsnorm(x, pre_ref[0]).astype(BF16)
    rc = min(m, CONV_CHUNK_ROWS)

    def taps(j, r0):
        lanes = slice(j * LANES, (j + 1) * LANES)
        acc = None
        for k in range(CONV_W):
            xs = xp_ref[j, pl.ds(r0 + (off0 - hp + k * bb), rc), :]
            term = xs.reshape(rc // SUBLANES, SUBLANES, LANES) * wb_ref[k, :, lanes][None]
            acc = term if acc is None else acc + term
        acc = acc + jnp.broadcast_to(b_dw_ref[0, :, lanes], (SUBLANES, LANES))[None]
        c_ref[pl.ds(r0, rc), lanes] = acc.reshape(rc, LANES)

    for c0 in range(0, c_dim, MXU_WIDTH):
        cu, cg = slice(c0, c0 + MXU_WIDTH), slice(c_dim + c0, c_dim + c0 + MXU_WIDTH)
        u = jnp.dot(hb, w_in_ref[0, :, cu], preferred_element_type=F32) + b_in_ref[0, :, cu]
        gt = jnp.dot(hb, w_in_ref[0, :, cg], preferred_element_type=F32) + b_in_ref[0, :, cg]
        v = u * jax.nn.sigmoid(gt)
        for jj in range(MXU_WIDTH // LANES):
            xp_ref[c0 // LANES + jj, off0:off0 + m, :] = v[:, jj * LANES:(jj + 1) * LANES]
        if bb == 1:
            for j in range(c0 // LANES, (c0 + MXU_WIDTH) // LANES):
                for r0 in range(0, m, rc):
                    taps(j, r0)

    if bb > 1:
        def chunk(i, carry):
            for j in range(nj):
                taps(j, pl.multiple_of(i * rc, rc))
            return carry

        lax.fori_loop(0, m // rc, chunk, 0)

    c = c_ref[...]
    mu = jnp.mean(c, axis=-1, keepdims=True)
    cc = c - mu
    var = jnp.mean(cc * cc, axis=-1, keepdims=True)
    yl = cc * lax.rsqrt(var + EPS) * ln_g_ref[0] + ln_b_ref[0]
    s = yl * jax.nn.sigmoid(yl)
    mo = jnp.dot(s.astype(BF16), w_out_ref[0], preferred_element_type=F32) + b_out_ref[0]
    _store_rows(y_ref, x + _rmsnorm(mo, post_ref[0]), bb)

    for j in range(nj):
        tail = xp_ref[j, off0 + m - hp:off0 + m, :]
        _hist_store(nst_ref, slice(j * LANES, (j + 1) * LANES), tail, bb)
        if carry:
            xp_ref[j, off0 - hp:off0, :] = tail


def _pool_mixer_kernel(x_ref, st_ref, pre_ref, w_grp_ref, scale_ref, post_ref,
                       y_ref, nst_ref, xp_ref, d_ref, *, bb, start_pos, carry):
    tm, d = _tile_dims(x_ref, bb)
    nj = d // LANES
    ng = len(POOL_WINDOWS)
    gd = d // ng
    m = tm * bb
    hp = (max(POOL_WINDOWS) - 1) * bb
    off0 = _round_up(hp, SUBLANES)
    t = pl.program_id(1)

    @pl.when(t == 0)
    def _():
        for j in range(nj):
            xp_ref[j, off0 - hp:off0, :] = _hist_load(st_ref, slice(j * LANES, (j + 1) * LANES), bb)

    x = _load_rows(x_ref, bb)
    h = _rmsnorm(x, pre_ref[0])
    for j in range(nj):
        xp_ref[j, off0:off0 + m, :] = h[:, j * LANES:(j + 1) * LANES]

    rc = min(m, CONV_CHUNK_ROWS)
    assert bb == 1 or bb % rc == 0
    full_windows = start_pos + 1 >= max(POOL_WINDOWS)
    for r0 in range(0, m, rc):
        if full_windows:
            pos = None
        elif bb == 1:
            pos = start_pos + t * tm + r0 + lax.broadcasted_iota(jnp.int32, (rc, LANES), 0)
        else:
            pos = jnp.full((rc, LANES), start_pos + r0 // bb, jnp.int32) + t * tm
        for g, w in enumerate(POOL_WINDOWS):
            cnt = float(w) if full_windows else jnp.minimum(w, pos + 1).astype(F32)
            for j in range(g * gd // LANES, (g + 1) * gd // LANES):
                win = None
                for k in range(w):
                    lo = off0 - k * bb + r0
                    xs = xp_ref[j, lo:lo + rc, :]
                    win = xs if win is None else win + xs
                cur = xp_ref[j, off0 + r0:off0 + r0 + rc, :]
                d_ref[r0:r0 + rc, j * LANES:(j + 1) * LANES] = win / cnt - cur

    dd = d_ref[...].astype(BF16)
    ys = [jnp.dot(dd[:, g * gd:(g + 1) * gd], w_grp_ref[0, g], preferred_element_type=F32)
          for g in range(ng)]
    y = jnp.concatenate(ys, axis=-1) * scale_ref[0]
    _store_rows(y_ref, x + _rmsnorm(y, post_ref[0]), bb)

    for j in range(nj):
        tail = xp_ref[j, off0 + m - hp:off0 + m, :]
        _hist_store(nst_ref, slice(j * LANES, (j + 1) * LANES), tail, bb)
        if carry:
            xp_ref[j, off0 - hp:off0, :] = tail


def _ffn_kernel(*refs, bb, final):
    if final:
        (x_ref, st_ref, pre_ref, w_up_ref, w_dw_ref, b_dw_ref, w_down_ref, post_ref, fin_ref,
         y_ref, nst_ref, slab_ref, cr_ref, act_ref) = refs
    else:
        (x_ref, st_ref, pre_ref, w_up_ref, w_dw_ref, b_dw_ref, w_down_ref, post_ref,
         y_ref, nst_ref, slab_ref, cr_ref, act_ref) = refs
    tm, d = _tile_dims(x_ref, bb)
    f = w_down_ref.shape[1]
    nc = MXU_WIDTH
    njc = nc // LANES
    m = tm * bb
    hp = (FFN_CONV_W - 1) * bb
    off0 = _round_up(hp, SUBLANES)
    t = pl.program_id(1)

    @pl.when(t == 0)
    def _():
        cr_ref[...] = _hist_load(st_ref, slice(None), bb)

    x = _load_rows(x_ref, bb)
    h = _rmsnorm(x, pre_ref[0]).astype(BF16)
    for c0 in range(0, f, nc):
        halves = []
        for hi, base in enumerate((c0, f + c0)):
            u = jnp.dot(h, w_up_ref[0, :, base:base + nc], preferred_element_type=F32)
            tiles = []
            for jj in range(njc):
                lanes = slice(base + jj * LANES, base + (jj + 1) * LANES)
                slot = hi * njc + jj
                u_t = u[:, jj * LANES:(jj + 1) * LANES]
                slab_ref[slot, off0 - hp:off0, :] = cr_ref[:, lanes]
                slab_ref[slot, off0:off0 + m, :] = u_t
                um2 = slab_ref[slot, off0 - 2 * bb:off0 - 2 * bb + m, :]
                um1 = slab_ref[slot, off0 - bb:off0 - bb + m, :]
                cr_ref[:, lanes] = slab_ref[slot, off0 + m - hp:off0 + m, :]
                tiles.append(um2 * w_dw_ref[0, 0:1, lanes] + um1 * w_dw_ref[0, 1:2, lanes]
                             + u_t * w_dw_ref[0, 2:3, lanes] + b_dw_ref[0, :, lanes])
            halves.append(jnp.concatenate(tiles, axis=-1))
        gate, val = halves
        act_ref[:, c0:c0 + nc] = (gate * jax.nn.sigmoid(gate) * val).astype(BF16)

    fo = jnp.dot(act_ref[...], w_down_ref[0], preferred_element_type=F32)
    out = x + _rmsnorm(fo, post_ref[0])
    if final:
        out = _rmsnorm(out, fin_ref[0])
    _store_rows(y_ref, out, bb)
    _hist_store(nst_ref, slice(None), cr_ref[...], bb)


def _layer_spec(arr, layer):
    nd = arr.ndim
    return pl.BlockSpec((1,) + arr.shape[1:], lambda b, t: (layer,) + (0,) * (nd - 1),
                        pipeline_mode=pl.Buffered(1))


def _skip_ref(kernel_fn, index):
    def wrapped(*refs):
        return kernel_fn(*refs[:index], *refs[index + 1:])
    return wrapped


_PARAMS = pltpu.CompilerParams(dimension_semantics=("arbitrary", "arbitrary"),
                               vmem_limit_bytes=VMEM_LIMIT_BYTES)


def _seq_call(kernel_fn, name, x, st_all, st_layer, prev_nst, consts, scratch, bb, tm):
    if bb == 1:
        bsz, t_len, d = x.shape
        n_l, st_b, p, c = st_all.shape
        x_spec = pl.BlockSpec((1, tm, d), lambda b, t: (b, t, 0))
        if st_b == bsz:
            st_spec = pl.BlockSpec((1, 1, p, c), lambda b, t: (st_layer, b, 0, 0))
        else:
            assert st_b == 1
            st_spec = pl.BlockSpec((1, 1, p, c), lambda b, t: (st_layer, 0, 0, 0))
        nst_spec = pl.BlockSpec((1, 1, p, c), lambda b, t: (st_layer, b, 0, 0))
        nst_shape = (n_l, bsz, p, c)
    else:
        t_len, bsz, d = x.shape
        n_l, p, st_b, c = st_all.shape
        assert st_b == bsz and bb % SUBLANES == 0
        x_spec = pl.BlockSpec((tm, bb, d), lambda b, t: (t, b, 0))
        st_spec = pl.BlockSpec((1, p, bb, c), lambda b, t: (st_layer, 0, b, 0))
        nst_spec = st_spec
        nst_shape = st_all.shape
    inputs = [x, st_all] + [a for a, _ in consts]
    in_specs = [x_spec, st_spec] + [_layer_spec(a, l) for a, l in consts]
    aliases = {}
    if prev_nst is not None:
        aliases = {len(inputs): 1}
        kernel_fn = _skip_ref(kernel_fn, len(inputs))
        inputs.append(prev_nst)
        in_specs.append(pl.BlockSpec(memory_space=pl.ANY))
    return pl.pallas_call(
        kernel_fn,
        grid=(bsz // bb, t_len // tm),
        in_specs=in_specs,
        out_specs=[x_spec, nst_spec],
        out_shape=[jax.ShapeDtypeStruct(x.shape, F32), jax.ShapeDtypeStruct(nst_shape, F32)],
        scratch_shapes=scratch,
        input_output_aliases=aliases,
        compiler_params=_PARAMS,
        name=name,
    )(*inputs)


def _seq_len(x, bb):
    return x.shape[1] if bb == 1 else x.shape[0]


def _conv_mixer(x, st_all, prev_nst, w, i, bb, tm):
    j = i // 2
    c_dim = w["a_w_out"].shape[1]
    m = tm * bb
    off0 = _round_up((CONV_W - 1) * bb, SUBLANES)
    consts = [(w["mix_pre"], i), (w["a_w_in"], j), (w["a_b_in"], j), (w["a_w_dw"], j),
              (w["a_b_dw"], j), (w["a_ln_g"], j), (w["a_ln_b"], j), (w["a_w_out"], j),
              (w["a_b_out"], j), (w["mix_post"], i)]
    scratch = [pltpu.VMEM((c_dim // LANES, off0 + m, LANES), F32),
               pltpu.VMEM((m, c_dim), F32),
               pltpu.VMEM((CONV_W, SUBLANES, c_dim), F32)]
    fn = functools.partial(_conv_mixer_kernel, bb=bb, carry=_seq_len(x, bb) > tm)
    return _seq_call(fn, "conv_mixer", x, st_all, j, prev_nst, consts, scratch, bb, tm)


def _pool_mixer(x, st_all, prev_nst, w, i, bb, tm, start_pos):
    j = i // 2
    d = x.shape[-1]
    m = tm * bb
    off0 = _round_up((max(POOL_WINDOWS) - 1) * bb, SUBLANES)
    consts = [(w["mix_pre"], i), (w["b_w_grp"], j), (w["b_scale"], j), (w["mix_post"], i)]
    scratch = [pltpu.VMEM((d // LANES, off0 + m, LANES), F32),
               pltpu.VMEM((m, d), F32)]
    fn = functools.partial(_pool_mixer_kernel, bb=bb, start_pos=start_pos,
                           carry=_seq_len(x, bb) > tm)
    return _seq_call(fn, "pool_mixer", x, st_all, j, prev_nst, consts, scratch, bb, tm)


def _conv_ffn(x, st_all, prev_nst, w, i, bb, tm, final):
    f = w["f_w_down"].shape[1]
    m = tm * bb
    hp = (FFN_CONV_W - 1) * bb
    off0 = _round_up(hp, SUBLANES)
    consts = [(w["ffn_pre"], i), (w["f_w_up"], i), (w["f_w_dw"], i), (w["f_b_dw"], i),
              (w["f_w_down"], i), (w["ffn_post"], i)]
    if final:
        consts.append((w["final_norm"], 0))
    scratch = [pltpu.VMEM((2 * MXU_WIDTH // LANES, off0 + m, LANES), F32),
               pltpu.VMEM((hp, 2 * f), F32),
               pltpu.VMEM((m, f), BF16)]
    fn = functools.partial(_ffn_kernel, bb=bb, final=final)
    return _seq_call(fn, "conv_ffn", x, st_all, i, prev_nst, consts, scratch, bb, tm)


def _trunk(x, st_conv, st_pool, st_ffn, start_pos, w, tiles):
    new_conv = new_pool = new_ffn = None
    depth = w["mix_pre"].shape[0]
    for i in range(depth):
        if i % 2 == 0:
            x, new_conv = _conv_mixer(x, st_conv, new_conv, w, i, *tiles["conv"])
        else:
            x, new_pool = _pool_mixer(x, st_pool, new_pool, w, i, *tiles["pool"], start_pos)
        x, new_ffn = _conv_ffn(x, st_ffn, new_ffn, w, i, *tiles["ffn"], i == depth - 1)
    return x, new_conv, new_pool, new_ffn


def kernel(x_prompt, x_sample, state_conv, state_pool, state_ffn, meta_tokens, mix_pre, mix_post,
           ffn_pre, ffn_post, final_norm, a_w_in, a_b_in, a_w_dw, a_b_dw, a_ln_g, a_ln_b, a_w_out,
           a_b_out, b_w_grp, b_scale, f_w_up, f_w_dw, f_b_dw, f_w_down):
    rows = lambda a: a.reshape(a.shape[0], 1, a.shape[-1])
    w = dict(mix_pre=rows(mix_pre), mix_post=rows(mix_post), ffn_pre=rows(ffn_pre),
             ffn_post=rows(ffn_post), final_norm=final_norm.reshape(1, 1, -1),
             a_w_in=a_w_in.astype(BF16), a_b_in=rows(a_b_in), a_w_dw=a_w_dw, a_b_dw=rows(a_b_dw),
             a_ln_g=rows(a_ln_g), a_ln_b=rows(a_ln_b), a_w_out=a_w_out.astype(BF16),
             a_b_out=rows(a_b_out), b_w_grp=b_w_grp.astype(BF16), b_scale=rows(b_scale),
             f_w_up=f_w_up.astype(BF16), f_w_dw=f_w_dw, f_b_dw=rows(f_b_dw),
             f_w_down=f_w_down.astype(BF16))
    t_dec = x_sample.shape[1]
    meta_tiles = dict(conv=(1, N_META), pool=(1, N_META), ffn=(1, N_META))
    prompt_tiles = dict(conv=(1, 1024), pool=(1, 1024), ffn=(1, 1024))
    sample_tiles = dict(conv=(32, t_dec), pool=(64, t_dec), ffn=(64, t_dec))

    zeros = lambda a: jnp.zeros((a.shape[0], 1) + a.shape[2:], F32)
    _, mc, mp, mf = _trunk(meta_tokens[None].astype(F32), zeros(state_conv), zeros(state_pool),
                           zeros(state_ffn), 0, w, meta_tiles)
    y_prompt, pc, ppl, pf = _trunk(x_prompt, mc, mp, mf, N_META, w, prompt_tiles)
    tmaj = lambda a: a.transpose(0, 2, 1, 3)
    y_tm, sc, sp, sf = _trunk(x_sample.transpose(1, 0, 2), tmaj(state_conv), tmaj(state_pool),
                              tmaj(state_ffn), PAST_LEN, w, sample_tiles)
    return (y_prompt, y_tm.transpose(1, 0, 2), pc, ppl, pf, tmaj(sc), tmaj(sp), tmaj(sf))
```

```python
import functools

import jax
import jax.numpy as jnp
from jax import lax
from jax.experimental import pallas as pl
from jax.experimental.pallas import tpu as pltpu

LANES = 128
SUBLANES = 8
MXU_WIDTH = 2 * LANES
VMEM_LIMIT_BYTES = 60 * 1024 * 1024

EPS = 1e-6
N_META = 16
PAST_LEN = 16384
CONV_W = 31
POOL_WINDOWS = (2, 4, 8, 16)
FFN_CONV_W = 3
CONV_CHUNK_ROWS = 32

F32 = jnp.float32
BF16 = jnp.bfloat16


def _round_up(n, m):
    return (n + m - 1) // m * m


def _rmsnorm(x, g):
    ms = jnp.mean(x * x, axis=-1, keepdims=True)
    return x * lax.rsqrt(ms + EPS) * g


def _tile_dims(x_ref, bb):
    return (x_ref.shape[1] if bb == 1 else x_ref.shape[0]), x_ref.shape[2]


def _load_rows(x_ref, bb):
    if bb == 1:
        return x_ref[0]
    tm, _, d = x_ref.shape
    return x_ref[...].reshape(tm * bb, d)


def _store_rows(y_ref, val, bb):
    if bb == 1:
        y_ref[0] = val
    else:
        y_ref[...] = val.reshape(y_ref.shape)


def _hist_load(st_ref, lanes, bb):
    if bb == 1:
        return st_ref[0, 0, :, lanes]
    h = st_ref[0, :, :, lanes]
    return h.reshape(h.shape[0] * bb, h.shape[2])


def _hist_store(nst_ref, lanes, val, bb):
    if bb == 1:
        nst_ref[0, 0, :, lanes] = val
    else:
        nst_ref[0, :, :, lanes] = val.reshape(val.shape[0] // bb, bb, val.shape[1])


def _conv_mixer_kernel(x_ref, st_ref, pre_ref, w_in_ref, b_in_ref, w_dw_ref, b_dw_ref,
                       ln_g_ref, ln_b_ref, w_out_ref, b_out_ref, post_ref,
                       y_ref, nst_ref, xp_ref, c_ref, wb_ref, *, bb, carry):
    tm, d = _tile_dims(x_ref, bb)
    c_dim = w_out_ref.shape[1]
    nj = c_dim // LANES
    m = tm * bb
    hp = (CONV_W - 1) * bb
    off0 = _round_up(hp, SUBLANES)
    t = pl.program_id(1)

    @pl.when((pl.program_id(0) == 0) & (t == 0))
    def _():
        for k in range(CONV_W):
            wb_ref[k] = jnp.broadcast_to(w_dw_ref[0, k:k + 1, :], (SUBLANES, c_dim))

    @pl.when(t == 0)
    def _():
        for j in range(nj):
            xp_ref[j, off0 - hp:off0, :] = _hist_load(st_ref, slice(j * LANES, (j + 1) * LANES), bb)

    x = _load_rows(x_ref, bb)
    hb = _rmsnorm(x, pre_ref[0]).astype(BF16)
    rc = min(m, CONV_CHUNK_ROWS)

    def taps(j, r0):
        lanes = slice(j * LANES, (j + 1) * LANES)
        acc = None
        for k in range(CONV_W):
            xs = xp_ref[j, pl.ds(r0 + (off0 - hp + k * bb), rc), :]
            term = xs.reshape(rc // SUBLANES, SUBLANES, LANES) * wb_ref[k, :, lanes][None]
            acc = term if acc is None else acc + term
        acc = acc + jnp.broadcast_to(b_dw_ref[0, :, lanes], (SUBLANES, LANES))[None]
        c_ref[pl.ds(r0, rc), lanes] = acc.reshape(rc, LANES)

    for c0 in range(0, c_dim, MXU_WIDTH):
        cu, cg = slice(c0, c0 + MXU_WIDTH), slice(c_dim + c0, c_dim + c0 + MXU_WIDTH)
        u = jnp.dot(hb, w_in_ref[0, :, cu], preferred_element_type=F32) + b_in_ref[0, :, cu]
        gt = jnp.dot(hb, w_in_ref[0, :, cg], preferred_element_type=F32) + b_in_ref[0, :, cg]
        v = u * jax.nn.sigmoid(gt)
        for jj in range(MXU_WIDTH // LANES):
            xp_ref[c0 // LANES + jj, off0:off0 + m, :] = v[:, jj * LANES:(jj + 1) * LANES]
        if bb == 1:
            for j in range(c0 // LANES, (c0 + MXU_WIDTH) // LANES):
                for r0 in range(0, m, rc):
                    taps(j, r0)

    if bb > 1:
        def chunk(i, carry):
            for j in range(nj):
                taps(j, pl.multiple_of(i * rc, rc))
            return carry

        lax.fori_loop(0, m // rc, chunk, 0)

    c = c_ref[...]
    mu = jnp.mean(c, axis=-1, keepdims=True)
    cc = c - mu
    var = jnp.mean(cc * cc, axis=-1, keepdims=True)
    yl = cc * lax.rsqrt(var + EPS) * ln_g_ref[0] + ln_b_ref[0]
    s = yl * jax.nn.sigmoid(yl)
    mo = jnp.dot(s.astype(BF16), w_out_ref[0], preferred_element_type=F32) + b_out_ref[0]
    _store_rows(y_ref, x + _rmsnorm(mo, post_ref[0]), bb)

    for j in range(nj):
        tail = xp_ref[j, off0 + m - hp:off0 + m, :]
        _hist_store(nst_ref, slice(j * LANES, (j + 1) * LANES), tail, bb)
        if carry:
            xp_ref[j, off0 - hp:off0, :] = tail


def _pool_mixer_kernel(x_ref, st_ref, pre_ref, w_grp_ref, scale_ref, post_ref,
                       y_ref, nst_ref, xp_ref, d_ref, *, bb, start_pos, carry):
    tm, d = _tile_dims(x_ref, bb)
    nj = d // LANES
    ng = len(POOL_WINDOWS)
    gd = d // ng
    m = tm * bb
    hp = (max(POOL_WINDOWS) - 1) * bb
    off0 = _round_up(hp, SUBLANES)
    t = pl.program_id(1)

    @pl.when(t == 0)
    def _():
        for j in range(nj):
            xp_ref[j, off0 - hp:off0, :] = _hist_load(st_ref, slice(j * LANES, (j + 1) * LANES), bb)

    x = _load_rows(x_ref, bb)
    h = _rmsnorm(x, pre_ref[0])
    for j in range(nj):
        xp_ref[j, off0:off0 + m, :] = h[:, j * LANES:(j + 1) * LANES]

    rc = min(m, CONV_CHUNK_ROWS)
    assert bb == 1 or bb % rc == 0
    full_windows = start_pos + 1 >= max(POOL_WINDOWS)
    for r0 in range(0, m, rc):
        if full_windows:
            pos = None
        elif bb == 1:
            pos = start_pos + t * tm + r0 + lax.broadcasted_iota(jnp.int32, (rc, LANES), 0)
        else:
            pos = jnp.full((rc, LANES), start_pos + r0 // bb, jnp.int32) + t * tm
        for g, w in enumerate(POOL_WINDOWS):
            cnt = float(w) if full_windows else jnp.minimum(w, pos + 1).astype(F32)
            for j in range(g * gd // LANES, (g + 1) * gd // LANES):
                win = None
                for k in range(w):
                    lo = off0 - k * bb + r0
                    xs = xp_ref[j, lo:lo + rc, :]
                    win = xs if win is None else win + xs
                cur = xp_ref[j, off0 + r0:off0 + r0 + rc, :]
                d_ref[r0:r0 + rc, j * LANES:(j + 1) * LANES] = win / cnt - cur

    dd = d_ref[...].astype(BF16)
    ys = [jnp.dot(dd[:, g * gd:(g + 1) * gd], w_grp_ref[0, g], preferred_element_type=F32)
          for g in range(ng)]
    y = jnp.concatenate(ys, axis=-1) * scale_ref[0]
    _store_rows(y_ref, x + _rmsnorm(y, post_ref[0]), bb)

    for j in range(nj):
        tail = xp_ref[j, off0 + m - hp:off0 + m, :]
        _hist_store(nst_ref, slice(j * LANES, (j + 1) * LANES), tail, bb)
        if carry:
            xp_ref[j, off0 - hp:off0, :] = tail


def _ffn_kernel(*refs, bb, final):
    if final:
        (x_ref, st_ref, pre_ref, w_up_ref, w_dw_ref, b_dw_ref, w_down_ref, post_ref, fin_ref,
         y_ref, nst_ref, slab_ref, cr_ref, act_ref) = refs
    else:
        (x_ref, st_ref, pre_ref, w_up_ref, w_dw_ref, b_dw_ref, w_down_ref, post_ref,
         y_ref, nst_ref, slab_ref, cr_ref, act_ref) = refs
    tm, d = _tile_dims(x_ref, bb)
    f = w_down_ref.shape[1]
    nc = MXU_WIDTH
    njc = nc // LANES
    m = tm * bb
    hp = (FFN_CONV_W - 1) * bb
    off0 = _round_up(hp, SUBLANES)
    t = pl.program_id(1)

    @pl.when(t == 0)
    def _():
        cr_ref[...] = _hist_load(st_ref, slice(None), bb)

    x = _load_rows(x_ref, bb)
    h = _rmsnorm(x, pre_ref[0]).astype(BF16)
    split = (f // nc // 2) * nc
    fo = None
    for c0 in range(0, f, nc):
        halves = []
        for hi, base in enumerate((c0, f + c0)):
            u = jnp.dot(h, w_up_ref[0, :, base:base + nc], preferred_element_type=F32)
            tiles = []
            for jj in range(njc):
                lanes = slice(base + jj * LANES, base + (jj + 1) * LANES)
                slot = hi * njc + jj
                u_t = u[:, jj * LANES:(jj + 1) * LANES]
                slab_ref[slot, off0 - hp:off0, :] = cr_ref[:, lanes]
                slab_ref[slot, off0:off0 + m, :] = u_t
                um2 = slab_ref[slot, off0 - 2 * bb:off0 - 2 * bb + m, :]
                um1 = slab_ref[slot, off0 - bb:off0 - bb + m, :]
                cr_ref[:, lanes] = slab_ref[slot, off0 + m - hp:off0 + m, :]
                tiles.append(um2 * w_dw_ref[0, 0:1, lanes] + um1 * w_dw_ref[0, 1:2, lanes]
                             + u_t * w_dw_ref[0, 2:3, lanes] + b_dw_ref[0, :, lanes])
            halves.append(jnp.concatenate(tiles, axis=-1))
        gate, val = halves
        act_ref[:, c0:c0 + nc] = (gate * jax.nn.sigmoid(gate) * val).astype(BF16)
        if c0 + nc == split:
            fo = jnp.dot(act_ref[:, :split], w_down_ref[0, :split, :], preferred_element_type=F32)

    fo = fo + jnp.dot(act_ref[:, split:], w_down_ref[0, split:, :], preferred_element_type=F32)
    out = x + _rmsnorm(fo, post_ref[0])
    if final:
        out = _rmsnorm(out, fin_ref[0])
    _store_rows(y_ref, out, bb)
    _hist_store(nst_ref, slice(None), cr_ref[...], bb)


def _layer_spec(arr, layer):
    nd = arr.ndim
    return pl.BlockSpec((1,) + arr.shape[1:], lambda b, t: (layer,) + (0,) * (nd - 1),
                        pipeline_mode=pl.Buffered(1))


def _skip_ref(kernel_fn, index):
    def wrapped(*refs):
        return kernel_fn(*refs[:index], *refs[index + 1:])
    return wrapped


_PARAMS = pltpu.CompilerParams(dimension_semantics=("arbitrary", "arbitrary"),
                               vmem_limit_bytes=VMEM_LIMIT_BYTES)


def _seq_call(kernel_fn, name, x, st_all, st_layer, prev_nst, consts, scratch, bb, tm):
    if bb == 1:
        bsz, t_len, d = x.shape
        n_l, st_b, p, c = st_all.shape
        x_spec = pl.BlockSpec((1, tm, d), lambda b, t: (b, t, 0))
        if st_b == bsz:
            st_spec = pl.BlockSpec((1, 1, p, c), lambda b, t: (st_layer, b, 0, 0))
        else:
            assert st_b == 1
            st_spec = pl.BlockSpec((1, 1, p, c), lambda b, t: (st_layer, 0, 0, 0))
        nst_spec = pl.BlockSpec((1, 1, p, c), lambda b, t: (st_layer, b, 0, 0))
        nst_shape = (n_l, bsz, p, c)
    else:
        t_len, bsz, d = x.shape
        n_l, p, st_b, c = st_all.shape
        assert st_b == bsz and bb % SUBLANES == 0
        x_spec = pl.BlockSpec((tm, bb, d), lambda b, t: (t, b, 0))
        st_spec = pl.BlockSpec((1, p, bb, c), lambda b, t: (st_layer, 0, b, 0))
        nst_spec = st_spec
        nst_shape = st_all.shape
    inputs = [x, st_all] + [a for a, _ in consts]
    in_specs = [x_spec, st_spec] + [_layer_spec(a, l) for a, l in consts]
    aliases = {}
    if prev_nst is not None:
        aliases = {len(inputs): 1}
        kernel_fn = _skip_ref(kernel_fn, len(inputs))
        inputs.append(prev_nst)
        in_specs.append(pl.BlockSpec(memory_space=pl.ANY))
    return pl.pallas_call(
        kernel_fn,
        grid=(bsz // bb, t_len // tm),
        in_specs=in_specs,
        out_specs=[x_spec, nst_spec],
        out_shape=[jax.ShapeDtypeStruct(x.shape, F32), jax.ShapeDtypeStruct(nst_shape, F32)],
        scratch_shapes=scratch,
        input_output_aliases=aliases,
        compiler_params=_PARAMS,
        name=name,
    )(*inputs)


def _seq_len(x, bb):
    return x.shape[1] if bb == 1 else x.shape[0]


def _conv_mixer(x, st_all, prev_nst, w, i, bb, tm):
    j = i // 2
    c_dim = w["a_w_out"].shape[1]
    m = tm * bb
    off0 = _round_up((CONV_W - 1) * bb, SUBLANES)
    consts = [(w["mix_pre"], i), (w["a_w_in"], j), (w["a_b_in"], j), (w["a_w_dw"], j),
              (w["a_b_dw"], j), (w["a_ln_g"], j), (w["a_ln_b"], j), (w["a_w_out"], j),
              (w["a_b_out"], j), (w["mix_post"], i)]
    scratch = [pltpu.VMEM((c_dim // LANES, off0 + m, LANES), F32),
               pltpu.VMEM((m, c_dim), F32),
               pltpu.VMEM((CONV_W, SUBLANES, c_dim), F32)]
    fn = functools.partial(_conv_mixer_kernel, bb=bb, carry=_seq_len(x, bb) > tm)
    return _seq_call(fn, "conv_mixer", x, st_all, j, prev_nst, consts, scratch, bb, tm)


def _pool_mixer(x, st_all, prev_nst, w, i, bb, tm, start_pos):
    j = i // 2
    d = x.shape[-1]
    m = tm * bb
    off0 = _round_up((max(POOL_WINDOWS) - 1) * bb, SUBLANES)
    consts = [(w["mix_pre"], i), (w["b_w_grp"], j), (w["b_scale"], j), (w["mix_post"], i)]
    scratch = [pltpu.VMEM((d // LANES, off0 + m, LANES), F32),
               pltpu.VMEM((m, d), F32)]
    fn = functools.partial(_pool_mixer_kernel, bb=bb, start_pos=start_pos,
                           carry=_seq_len(x, bb) > tm)
    return _seq_call(fn, "pool_mixer", x, st_all, j, prev_nst, consts, scratch, bb, tm)


def _conv_ffn(x, st_all, prev_nst, w, i, bb, tm, final):
    f = w["f_w_down"].shape[1]
    m = tm * bb
    hp = (FFN_CONV_W - 1) * bb
    off0 = _round_up(hp, SUBLANES)
    consts = [(w["ffn_pre"], i), (w["f_w_up"], i), (w["f_w_dw"], i), (w["f_b_dw"], i),
              (w["f_w_down"], i), (w["ffn_post"], i)]
    if final:
        consts.append((w["final_norm"], 0))
    scratch = [pltpu.VMEM((2 * MXU_WIDTH // LANES, off0 + m, LANES), F32),
               pltpu.VMEM((hp, 2 * f), F32),
               pltpu.VMEM((m, f), BF16)]
    fn = functools.partial(_ffn_kernel, bb=bb, final=final)
    return _seq_call(fn, "conv_ffn", x, st_all, i, prev_nst, consts, scratch, bb, tm)


def _trunk(x, st_conv, st_pool, st_ffn, start_pos, w, tiles):
    new_conv = new_pool = new_ffn = None
    depth = w["mix_pre"].shape[0]
    for i in range(depth):
        if i % 2 == 0:
            x, new_conv = _conv_mixer(x, st_conv, new_conv, w, i, *tiles["conv"])
        else:
            x, new_pool = _pool_mixer(x, st_pool, new_pool, w, i, *tiles["pool"], start_pos)
        x, new_ffn = _conv_ffn(x, st_ffn, new_ffn, w, i, *tiles["ffn"], i == depth - 1)
    return x, new_conv, new_pool, new_ffn


def kernel(x_prompt, x_sample, state_conv, state_pool, state_ffn, meta_tokens, mix_pre, mix_post,
           ffn_pre, ffn_post, final_norm, a_w_in, a_b_in, a_w_dw, a_b_dw, a_ln_g, a_ln_b, a_w_out,
           a_b_out, b_w_grp, b_scale, f_w_up, f_w_dw, f_b_dw, f_w_down):
    rows = lambda a: a.reshape(a.shape[0], 1, a.shape[-1])
    w = dict(mix_pre=rows(mix_pre), mix_post=rows(mix_post), ffn_pre=rows(ffn_pre),
             ffn_post=rows(ffn_post), final_norm=final_norm.reshape(1, 1, -1),
             a_w_in=a_w_in.astype(BF16), a_b_in=rows(a_b_in), a_w_dw=a_w_dw, a_b_dw=rows(a_b_dw),
             a_ln_g=rows(a_ln_g), a_ln_b=rows(a_ln_b), a_w_out=a_w_out.astype(BF16),
             a_b_out=rows(a_b_out), b_w_grp=b_w_grp.astype(BF16), b_scale=rows(b_scale),
             f_w_up=f_w_up.astype(BF16), f_w_dw=f_w_dw, f_b_dw=rows(f_b_dw),
             f_w_down=f_w_down.astype(BF16))
    t_dec = x_sample.shape[1]
    meta_tiles = dict(conv=(1, N_META), pool=(1, N_META), ffn=(1, N_META))
    prompt_tiles = dict(conv=(1, 1024), pool=(1, 1024), ffn=(1, 1024))
    sample_tiles = dict(conv=(32, t_dec), pool=(64, t_dec), ffn=(64, t_dec))

    zeros = lambda a: jnp.zeros((a.shape[0], 1) + a.shape[2:], F32)
    _, mc, mp, mf = _trunk(meta_tokens[None].astype(F32), zeros(state_conv), zeros(state_pool),
                           zeros(state_ffn), 0, w, meta_tiles)
    y_prompt, pc, ppl, pf = _trunk(x_prompt, mc, mp, mf, N_META, w, prompt_tiles)
    tmaj = lambda a: a.transpose(0, 2, 1, 3)
    y_tm, sc, sp, sf = _trunk(x_sample.transpose(1, 0, 2), tmaj(state_conv), tmaj(state_pool),
                              tmaj(state_ffn), PAST_LEN, w, sample_tiles)
    return (y_prompt, y_tm.transpose(1, 0, 2), pc, ppl, pf, tmaj(sc), tmaj(sp), tmaj(sf))
```

```python
import functools

import jax
import jax.numpy as jnp
from jax import lax
from jax.experimental import pallas as pl
from jax.experimental.pallas import tpu as pltpu

LANES = 128
SUBLANES = 8
MXU_WIDTH = 2 * LANES
VMEM_LIMIT_BYTES = 60 * 1024 * 1024

EPS = 1e-6
N_META = 16
PAST_LEN = 16384
CONV_W = 31
POOL_WINDOWS = (2, 4, 8, 16)
FFN_CONV_W = 3
CONV_CHUNK_ROWS = 32

F32 = jnp.float32
BF16 = jnp.bfloat16


def _round_up(n, m):
    return (n + m - 1) // m * m


def _rmsnorm(x, g):
    ms = jnp.mean(x * x, axis=-1, keepdims=True)
    return x * lax.rsqrt(ms + EPS) * g


def _tile_dims(x_ref, bb):
    return (x_ref.shape[1] if bb == 1 else x_ref.shape[0]), x_ref.shape[2]


def _load_rows(x_ref, bb):
    if bb == 1:
        return x_ref[0]
    tm, _, d = x_ref.shape
    return x_ref[...].reshape(tm * bb, d)


def _store_rows(y_ref, val, bb):
    if bb == 1:
        y_ref[0] = val
    else:
        y_ref[...] = val.reshape(y_ref.shape)


def _hist_load(st_ref, lanes, bb):
    if bb == 1:
        return st_ref[0, 0, :, lanes]
    h = st_ref[0, :, :, lanes]
    return h.reshape(h.shape[0] * bb, h.shape[2])


def _hist_store(nst_ref, lanes, val, bb):
    if bb == 1:
        nst_ref[0, 0, :, lanes] = val
    else:
        nst_ref[0, :, :, lanes] = val.reshape(val.shape[0] // bb, bb, val.shape[1])


def _conv_mixer_kernel(x_ref, st_ref, pre_ref, w_in_ref, b_in_ref, w_dw_ref, b_dw_ref,
                       ln_g_ref, ln_b_ref, w_out_ref, b_out_ref, post_ref,
                       y_ref, nst_ref, xp_ref, c_ref, wb_ref, *, bb, carry):
    tm, d = _tile_dims(x_ref, bb)
    c_dim = w_out_ref.shape[1]
    nj = c_dim // LANES
    m = tm * bb
    hp = (CONV_W - 1) * bb
    off0 = _round_up(hp, SUBLANES)
    t = pl.program_id(1)

    @pl.when((pl.program_id(0) == 0) & (t == 0))
    def _():
        for k in range(CONV_W):
            wb_ref[k] = jnp.broadcast_to(w_dw_ref[0, k:k + 1, :], (SUBLANES, c_dim))

    @pl.when(t == 0)
    def _():
        for j in range(nj):
            xp_ref[j, off0 - hp:off0, :] = _hist_load(st_ref, slice(j * LANES, (j + 1) * LANES), bb)

    x = _load_rows(x_ref, bb)
    hb = _rmsnorm(x, pre_ref[0]).astype(BF16)
    rc = min(m, CONV_CHUNK_ROWS)

    def taps(j, r0):
        lanes = slice(j * LANES, (j + 1) * LANES)
        acc = None
        for k in range(CONV_W):
            xs = xp_ref[j, pl.ds(r0 + (off0 - hp + k * bb), rc), :]
            term = xs.reshape(rc // SUBLANES, SUBLANES, LANES) * wb_ref[k, :, lanes][None]
            acc = term if acc is None else acc + term
        acc = acc + jnp.broadcast_to(b_dw_ref[0, :, lanes], (SUBLANES, LANES))[None]
        c_ref[pl.ds(r0, rc), lanes] = acc.reshape(rc, LANES)

    for c0 in range(0, c_dim, MXU_WIDTH):
        cu, cg = slice(c0, c0 + MXU_WIDTH), slice(c_dim + c0, c_dim + c0 + MXU_WIDTH)
        u = jnp.dot(hb, w_in_ref[0, :, cu], preferred_element_type=F32) + b_in_ref[0, :, cu]
        gt = jnp.dot(hb, w_in_ref[0, :, cg], preferred_element_type=F32) + b_in_ref[0, :, cg]
        v = u * jax.nn.sigmoid(gt)
        for jj in range(MXU_WIDTH // LANES):
            xp_ref[c0 // LANES + jj, off0:off0 + m, :] = v[:, jj * LANES:(jj + 1) * LANES]
        if bb == 1:
            for j in range(c0 // LANES, (c0 + MXU_WIDTH) // LANES):
                for r0 in range(0, m, rc):
                    taps(j, r0)

    if bb > 1:
        def chunk(i, carry):
            for j in range(nj):
                taps(j, pl.multiple_of(i * rc, rc))
            return carry

        lax.fori_loop(0, m // rc, chunk, 0)

    mh = 512 if (bb == 1 and m % 512 == 0) else m
    for h0 in range(0, m, mh):
        c = c_ref[h0:h0 + mh, :]
        mu = jnp.mean(c, axis=-1, keepdims=True)
        cc = c - mu
        var = jnp.mean(cc * cc, axis=-1, keepdims=True)
        yl = cc * lax.rsqrt(var + EPS) * ln_g_ref[0] + ln_b_ref[0]
        s = yl * jax.nn.sigmoid(yl)
        mo = jnp.dot(s.astype(BF16), w_out_ref[0], preferred_element_type=F32) + b_out_ref[0]
        out = x[h0:h0 + mh, :] + _rmsnorm(mo, post_ref[0])
        if bb == 1:
            y_ref[0, h0:h0 + mh, :] = out
        else:
            _store_rows(y_ref, out, bb)

    for j in range(nj):
        tail = xp_ref[j, off0 + m - hp:off0 + m, :]
        _hist_store(nst_ref, slice(j * LANES, (j + 1) * LANES), tail, bb)
        if carry:
            xp_ref[j, off0 - hp:off0, :] = tail


def _pool_mixer_kernel(x_ref, st_ref, pre_ref, w_grp_ref, scale_ref, post_ref,
                       y_ref, nst_ref, xp_ref, d_ref, *, bb, start_pos, carry):
    tm, d = _tile_dims(x_ref, bb)
    nj = d // LANES
    ng = len(POOL_WINDOWS)
    gd = d // ng
    m = tm * bb
    hp = (max(POOL_WINDOWS) - 1) * bb
    off0 = _round_up(hp, SUBLANES)
    t = pl.program_id(1)

    @pl.when(t == 0)
    def _():
        for j in range(nj):
            xp_ref[j, off0 - hp:off0, :] = _hist_load(st_ref, slice(j * LANES, (j + 1) * LANES), bb)

    x = _load_rows(x_ref, bb)
    h = _rmsnorm(x, pre_ref[0])
    for j in range(nj):
        xp_ref[j, off0:off0 + m, :] = h[:, j * LANES:(j + 1) * LANES]

    rc = min(m, CONV_CHUNK_ROWS)
    assert bb == 1 or bb % rc == 0
    full_windows = start_pos + 1 >= max(POOL_WINDOWS)
    for r0 in range(0, m, rc):
        if full_windows:
            pos = None
        elif bb == 1:
            pos = start_pos + t * tm + r0 + lax.broadcasted_iota(jnp.int32, (rc, LANES), 0)
        else:
            pos = jnp.full((rc, LANES), start_pos + r0 // bb, jnp.int32) + t * tm
        for g, w in enumerate(POOL_WINDOWS):
            cnt = float(w) if full_windows else jnp.minimum(w, pos + 1).astype(F32)
            for j in range(g * gd // LANES, (g + 1) * gd // LANES):
                win = None
                for k in range(w):
                    lo = off0 - k * bb + r0
                    xs = xp_ref[j, lo:lo + rc, :]
                    win = xs if win is None else win + xs
                cur = xp_ref[j, off0 + r0:off0 + r0 + rc, :]
                d_ref[r0:r0 + rc, j * LANES:(j + 1) * LANES] = win / cnt - cur

    dd = d_ref[...].astype(BF16)
    ys = [jnp.dot(dd[:, g * gd:(g + 1) * gd], w_grp_ref[0, g], preferred_element_type=F32)
          for g in range(ng)]
    y = jnp.concatenate(ys, axis=-1) * scale_ref[0]
    _store_rows(y_ref, x + _rmsnorm(y, post_ref[0]), bb)

    for j in range(nj):
        tail = xp_ref[j, off0 + m - hp:off0 + m, :]
        _hist_store(nst_ref, slice(j * LANES, (j + 1) * LANES), tail, bb)
        if carry:
            xp_ref[j, off0 - hp:off0, :] = tail


def _ffn_kernel(*refs, bb, final):
    if final:
        (x_ref, st_ref, pre_ref, w_up_ref, w_dw_ref, b_dw_ref, w_down_ref, post_ref, fin_ref,
         y_ref, nst_ref, slab_ref, cr_ref, act_ref) = refs
    else:
        (x_ref, st_ref, pre_ref, w_up_ref, w_dw_ref, b_dw_ref, w_down_ref, post_ref,
         y_ref, nst_ref, slab_ref, cr_ref, act_ref) = refs
    tm, d = _tile_dims(x_ref, bb)
    f = w_down_ref.shape[1]
    nc = MXU_WIDTH
    njc = nc // LANES
    m = tm * bb
    hp = (FFN_CONV_W - 1) * bb
    off0 = _round_up(hp, SUBLANES)
    t = pl.program_id(1)

    @pl.when(t == 0)
    def _():
        cr_ref[...] = _hist_load(st_ref, slice(None), bb)

    x = _load_rows(x_ref, bb)
    h = _rmsnorm(x, pre_ref[0]).astype(BF16)
    for c0 in range(0, f, nc):
        halves = []
        for hi, base in enumerate((c0, f + c0)):
            u = jnp.dot(h, w_up_ref[0, :, base:base + nc], preferred_element_type=F32)
            tiles = []
            for jj in range(njc):
                lanes = slice(base + jj * LANES, base + (jj + 1) * LANES)
                slot = hi * njc + jj
                u_t = u[:, jj * LANES:(jj + 1) * LANES]
                slab_ref[slot, off0 - hp:off0, :] = cr_ref[:, lanes]
                slab_ref[slot, off0:off0 + m, :] = u_t
                um2 = slab_ref[slot, off0 - 2 * bb:off0 - 2 * bb + m, :]
                um1 = slab_ref[slot, off0 - bb:off0 - bb + m, :]
                cr_ref[:, lanes] = slab_ref[slot, off0 + m - hp:off0 + m, :]
                tiles.append(um2 * w_dw_ref[0, 0:1, lanes] + um1 * w_dw_ref[0, 1:2, lanes]
                             + u_t * w_dw_ref[0, 2:3, lanes] + b_dw_ref[0, :, lanes])
            halves.append(jnp.concatenate(tiles, axis=-1))
        gate, val = halves
        act_ref[:, c0:c0 + nc] = (gate * jax.nn.sigmoid(gate) * val).astype(BF16)

    fo = jnp.dot(act_ref[...], w_down_ref[0], preferred_element_type=F32)
    out = x + _rmsnorm(fo, post_ref[0])
    if final:
        out = _rmsnorm(out, fin_ref[0])
    _store_rows(y_ref, out, bb)
    _hist_store(nst_ref, slice(None), cr_ref[...], bb)


def _layer_spec(arr, layer):
    nd = arr.ndim
    return pl.BlockSpec((1,) + arr.shape[1:], lambda b, t: (layer,) + (0,) * (nd - 1),
                        pipeline_mode=pl.Buffered(1))


def _skip_ref(kernel_fn, index):
    def wrapped(*refs):
        return kernel_fn(*refs[:index], *refs[index + 1:])
    return wrapped


_PARAMS = pltpu.CompilerParams(dimension_semantics=("arbitrary", "arbitrary"),
                               vmem_limit_bytes=VMEM_LIMIT_BYTES)


def _seq_call(kernel_fn, name, x, st_all, st_layer, prev_nst, consts, scratch, bb, tm):
    if bb == 1:
        bsz, t_len, d = x.shape
        n_l, st_b, p, c = st_all.shape
        x_spec = pl.BlockSpec((1, tm, d), lambda b, t: (b, t, 0))
        if st_b == bsz:
            st_spec = pl.BlockSpec((1, 1, p, c), lambda b, t: (st_layer, b, 0, 0))
        else:
            assert st_b == 1
            st_spec = pl.BlockSpec((1, 1, p, c), lambda b, t: (st_layer, 0, 0, 0))
        nst_spec = pl.BlockSpec((1, 1, p, c), lambda b, t: (st_layer, b, 0, 0))
        nst_shape = (n_l, bsz, p, c)
    else:
        t_len, bsz, d = x.shape
        n_l, p, st_b, c = st_all.shape
        assert st_b == bsz and bb % SUBLANES == 0
        x_spec = pl.BlockSpec((tm, bb, d), lambda b, t: (t, b, 0))
        st_spec = pl.BlockSpec((1, p, bb, c), lambda b, t: (st_layer, 0, b, 0))
        nst_spec = st_spec
        nst_shape = st_all.shape
    inputs = [x, st_all] + [a for a, _ in consts]
    in_specs = [x_spec, st_spec] + [_layer_spec(a, l) for a, l in consts]
    aliases = {}
    if prev_nst is not None:
        aliases = {len(inputs): 1}
        kernel_fn = _skip_ref(kernel_fn, len(inputs))
        inputs.append(prev_nst)
        in_specs.append(pl.BlockSpec(memory_space=pl.ANY))
    return pl.pallas_call(
        kernel_fn,
        grid=(bsz // bb, t_len // tm),
        in_specs=in_specs,
        out_specs=[x_spec, nst_spec],
        out_shape=[jax.ShapeDtypeStruct(x.shape, F32), jax.ShapeDtypeStruct(nst_shape, F32)],
        scratch_shapes=scratch,
        input_output_aliases=aliases,
        compiler_params=_PARAMS,
        name=name,
    )(*inputs)


def _seq_len(x, bb):
    return x.shape[1] if bb == 1 else x.shape[0]


def _conv_mixer(x, st_all, prev_nst, w, i, bb, tm):
    j = i // 2
    c_dim = w["a_w_out"].shape[1]
    m = tm * bb
    off0 = _round_up((CONV_W - 1) * bb, SUBLANES)
    consts = [(w["mix_pre"], i), (w["a_w_in"], j), (w["a_b_in"], j), (w["a_w_dw"], j),
              (w["a_b_dw"], j), (w["a_ln_g"], j), (w["a_ln_b"], j), (w["a_w_out"], j),
              (w["a_b_out"], j), (w["mix_post"], i)]
    scratch = [pltpu.VMEM((c_dim // LANES, off0 + m, LANES), F32),
               pltpu.VMEM((m, c_dim), F32),
               pltpu.VMEM((CONV_W, SUBLANES, c_dim), F32)]
    fn = functools.partial(_conv_mixer_kernel, bb=bb, carry=_seq_len(x, bb) > tm)
    return _seq_call(fn, "conv_mixer", x, st_all, j, prev_nst, consts, scratch, bb, tm)


def _pool_mixer(x, st_all, prev_nst, w, i, bb, tm, start_pos):
    j = i // 2
    d = x.shape[-1]
    m = tm * bb
    off0 = _round_up((max(POOL_WINDOWS) - 1) * bb, SUBLANES)
    consts = [(w["mix_pre"], i), (w["b_w_grp"], j), (w["b_scale"], j), (w["mix_post"], i)]
    scratch = [pltpu.VMEM((d // LANES, off0 + m, LANES), F32),
               pltpu.VMEM((m, d), F32)]
    fn = functools.partial(_pool_mixer_kernel, bb=bb, start_pos=start_pos,
                           carry=_seq_len(x, bb) > tm)
    return _seq_call(fn, "pool_mixer", x, st_all, j, prev_nst, consts, scratch, bb, tm)


def _conv_ffn(x, st_all, prev_nst, w, i, bb, tm, final):
    f = w["f_w_down"].shape[1]
    m = tm * bb
    hp = (FFN_CONV_W - 1) * bb
    off0 = _round_up(hp, SUBLANES)
    consts = [(w["ffn_pre"], i), (w["f_w_up"], i), (w["f_w_dw"], i), (w["f_b_dw"], i),
              (w["f_w_down"], i), (w["ffn_post"], i)]
    if final:
        consts.append((w["final_norm"], 0))
    scratch = [pltpu.VMEM((2 * MXU_WIDTH // LANES, off0 + m, LANES), F32),
               pltpu.VMEM((hp, 2 * f), F32),
               pltpu.VMEM((m, f), BF16)]
    fn = functools.partial(_ffn_kernel, bb=bb, final=final)
    return _seq_call(fn, "conv_ffn", x, st_all, i, prev_nst, consts, scratch, bb, tm)


def _trunk(x, st_conv, st_pool, st_ffn, start_pos, w, tiles):
    new_conv = new_pool = new_ffn = None
    depth = w["mix_pre"].shape[0]
    for i in range(depth):
        if i % 2 == 0:
            x, new_conv = _conv_mixer(x, st_conv, new_conv, w, i, *tiles["conv"])
        else:
            x, new_pool = _pool_mixer(x, st_pool, new_pool, w, i, *tiles["pool"], start_pos)
        x, new_ffn = _conv_ffn(x, st_ffn, new_ffn, w, i, *tiles["ffn"], i == depth - 1)
    return x, new_conv, new_pool, new_ffn


def kernel(x_prompt, x_sample, state_conv, state_pool, state_ffn, meta_tokens, mix_pre, mix_post,
           ffn_pre, ffn_post, final_norm, a_w_in, a_b_in, a_w_dw, a_b_dw, a_ln_g, a_ln_b, a_w_out,
           a_b_out, b_w_grp, b_scale, f_w_up, f_w_dw, f_b_dw, f_w_down):
    rows = lambda a: a.reshape(a.shape[0], 1, a.shape[-1])
    w = dict(mix_pre=rows(mix_pre), mix_post=rows(mix_post), ffn_pre=rows(ffn_pre),
             ffn_post=rows(ffn_post), final_norm=final_norm.reshape(1, 1, -1),
             a_w_in=a_w_in.astype(BF16), a_b_in=rows(a_b_in), a_w_dw=a_w_dw, a_b_dw=rows(a_b_dw),
             a_ln_g=rows(a_ln_g), a_ln_b=rows(a_ln_b), a_w_out=a_w_out.astype(BF16),
             a_b_out=rows(a_b_out), b_w_grp=b_w_grp.astype(BF16), b_scale=rows(b_scale),
             f_w_up=f_w_up.astype(BF16), f_w_dw=f_w_dw, f_b_dw=rows(f_b_dw),
             f_w_down=f_w_down.astype(BF16))
    t_dec = x_sample.shape[1]
    meta_tiles = dict(conv=(1, N_META), pool=(1, N_META), ffn=(1, N_META))
    prompt_tiles = dict(conv=(1, 1024), pool=(1, 1024), ffn=(1, 1024))
    sample_tiles = dict(conv=(32, t_dec), pool=(64, t_dec), ffn=(64, t_dec))

    zeros = lambda a: jnp.zeros((a.shape[0], 1) + a.shape[2:], F32)
    _, mc, mp, mf = _trunk(meta_tokens[None].astype(F32), zeros(state_conv), zeros(state_pool),
                           zeros(state_ffn), 0, w, meta_tiles)
    y_prompt, pc, ppl, pf = _trunk(x_prompt, mc, mp, mf, N_META, w, prompt_tiles)
    tmaj = lambda a: a.transpose(0, 2, 1, 3)
    y_tm, sc, sp, sf = _trunk(x_sample.transpose(1, 0, 2), tmaj(state_conv), tmaj(state_pool),
                              tmaj(state_ffn), PAST_LEN, w, sample_tiles)
    return (y_prompt, y_tm.transpose(1, 0, 2), pc, ppl, pf, tmaj(sc), tmaj(sp), tmaj(sf))
```
